```python
import math
import jax, jax.numpy as jnp
from jax import lax
import numpy as np

D_MODEL = 1024
BATCH = 4
SEQ = 4096
DEPTH = 2
DEC_BATCH = 128
DEC_SEQ = 1
PAST_LEN = 16384
PAGE_SIZE = 128

HEAD_DIM = 64
SB_HEADS = 8
SB_KV_HEADS = 2
MLA_HEADS = 8
MLA_NOPE = 64
MLA_ROPE = 32
MLA_V = 64
KV_LORA = 128
MOBA_HEADS = 8
MOBA_KV_HEADS = 2
MOBA_BLOCK = 256
MOBA_TOPK = 3
N_BRANCH = 3
D_FF = 2816
ROPE_THETA = 10000.0
EPS = 1e-6
Q_BLOCK = 128
MOBA_Q_BLOCK = 64
SPLIT_SIZES = (SB_HEADS * HEAD_DIM, SB_KV_HEADS * HEAD_DIM, SB_KV_HEADS * HEAD_DIM,
               MLA_HEADS * (MLA_NOPE + MLA_ROPE), KV_LORA, MLA_ROPE,
               MOBA_HEADS * HEAD_DIM, MOBA_KV_HEADS * HEAD_DIM, MOBA_KV_HEADS * HEAD_DIM,
               N_BRANCH * D_MODEL)
D_IN = sum(SPLIT_SIZES)

kernel_name = "hybrid_sb_mla_moba_macaron_step"


def _rmsnorm(x, g):
    xf = x.astype(jnp.float32)
    y = xf * lax.rsqrt(jnp.mean(xf * xf, axis=-1, keepdims=True) + EPS) * g.astype(jnp.float32)
    return y.astype(x.dtype)


def _rope(x, pos):
    d = x.shape[-1]
    half = d // 2
    inv = jnp.exp(-math.log(ROPE_THETA) * 2.0 * jnp.arange(half, dtype=jnp.float32) / d)
    ang = pos.astype(jnp.float32)[:, None] * inv[None, :]
    cos = jnp.cos(ang)[None, :, None, :]
    sin = jnp.sin(ang)[None, :, None, :]
    xf = x.astype(jnp.float32)
    x1, x2 = xf[..., :half], xf[..., half:]
    return jnp.concatenate([x1 * cos - x2 * sin, x2 * cos + x1 * sin], axis=-1).astype(x.dtype)


def _swiglu(h, w_gu, w_down):
    a, b = jnp.split(h @ w_gu, 2, axis=-1)
    return (jax.nn.silu(a) * b) @ w_down


def _sweep(fn, block, qpos, q_args, kv_args):
    T = qpos.shape[0]
    if T > block and T % block == 0:
        n = T // block
        def split(a):
            return jnp.moveaxis(a.reshape(a.shape[0], n, block, *a.shape[2:]), 1, 0)
        def body(xs):
            qs, p = xs
            return fn(*qs, p, *kv_args)
        out = lax.map(body, (tuple(split(a) for a in q_args), qpos.reshape(n, block)))
        out = jnp.moveaxis(out, 0, 1)
        return out.reshape(out.shape[0], T, *out.shape[3:])
    return fn(*q_args, qpos, *kv_args)


def _sb_block(q, qpos, k, v):
    B, C, H, D = q.shape
    L, KVH = k.shape[1], k.shape[2]
    qg = q.reshape(B, C, KVH, H // KVH, D)
    z = jnp.einsum('bcngd,blnd->bngcl', qg, k).astype(jnp.float32) * (D ** -0.5)
    mask = jnp.arange(L)[None, :] < qpos[:, None]
    log_beta = jax.nn.log_sigmoid(z)
    log_keep = jnp.where(mask, jax.nn.log_sigmoid(-z), 0.0)
    after = lax.cumsum(log_keep, axis=4, reverse=True) - log_keep
    a = jnp.where(mask, jnp.exp(log_beta + after), 0.0)
    o = jnp.einsum('bngcl,blnd->bcngd', a.astype(v.dtype), v)
    return o.reshape(B, C, H, D)


def _mla_block(q_lat, q_rope, qpos, ckv, krope):
    L = ckv.shape[1]
    s = (jnp.einsum('bchr,blr->bhcl', q_lat, ckv) +
         jnp.einsum('bchd,bld->bhcl', q_rope, krope)).astype(jnp.float32)
    s = s * ((MLA_NOPE + MLA_ROPE) ** -0.5)
    mask = jnp.arange(L)[None, :] <= qpos[:, None]
    p = jax.nn.softmax(jnp.where(mask, s, -jnp.inf), axis=-1)
    return jnp.einsum('bhcl,blr->bchr', p.astype(ckv.dtype), ckv)


def _moba_blocks(k, v):
    B, L, n, d = k.shape
    pad = (-L) % MOBA_BLOCK
    nb = (L + pad) // MOBA_BLOCK
    kb = jnp.pad(k, ((0, 0), (0, pad), (0, 0), (0, 0))).reshape(B, nb, MOBA_BLOCK, n, d)
    vb = jnp.pad(v, ((0, 0), (0, pad), (0, 0), (0, 0))).reshape(B, nb, MOBA_BLOCK, n, d)
    kmean = jnp.mean(kb.astype(jnp.float32), axis=2)
    return kb.transpose(0, 3, 1, 2, 4), vb.transpose(0, 3, 1, 2, 4), kmean


def _moba_block(q, qpos, kbt, vbt, kmean):
    B, C, H, D = q.shape
    KVH, NB = kbt.shape[1], kbt.shape[2]
    G = H // KVH
    qg = q.reshape(B, C, KVH, G, D)
    gate = jnp.einsum('bcngd,bmnd->bngcm', qg.astype(jnp.float32), kmean)
    own = qpos // MOBA_BLOCK
    past_ok = jnp.arange(NB)[None, :] < own[:, None]
    gate = jnp.where(past_ok, gate, -jnp.inf)
    k_sel = min(MOBA_TOPK, NB)
    _, top = lax.top_k(gate, k_sel)
    own_b = jnp.broadcast_to(own[:, None], top.shape[:-1] + (1,)).astype(top.dtype)
    sel = jnp.concatenate([top, own_b], axis=-1)
    is_own = jnp.arange(k_sel + 1) == k_sel
    slot_ok = (sel < own[:, None]) | is_own
    kpos = sel[..., None] * MOBA_BLOCK + jnp.arange(MOBA_BLOCK)
    mask = slot_ok[..., None] & (kpos <= qpos[:, None, None])
    bi = jnp.arange(B)[:, None, None, None, None]
    ni = jnp.arange(KVH)[None, :, None, None, None]
    kg = kbt[bi, ni, sel]
    vg = vbt[bi, ni, sel]
    s = jnp.einsum('bcngd,bngcsjd->bngcsj', qg, kg).astype(jnp.float32) * (D ** -0.5)
    s = jnp.where(mask, s, -jnp.inf)
    p = jax.nn.softmax(s.reshape(*s.shape[:4], -1), axis=-1).reshape(s.shape)
    o = jnp.einsum('bngcsj,bngcsjd->bcngd', p.astype(vg.dtype), vg)
    return o.reshape(B, C, H, D)


def _mixer(h, pos, past, w_in, g_kv, w_uk, w_uv, w_o_sb, w_o_mla, w_o_moba, w_out):
    B, T, _ = h.shape
    offsets = [int(o) for o in np.cumsum(SPLIT_SIZES)[:-1]]
    sb_q, sb_k, sb_v, mla_q, ckv, kr, mb_q, mb_k, mb_v, gates = jnp.split(h @ w_in, offsets, axis=-1)
    sb_q = sb_q.reshape(B, T, SB_HEADS, HEAD_DIM)
    sb_k = sb_k.reshape(B, T, SB_KV_HEADS, HEAD_DIM)
    sb_v = sb_v.reshape(B, T, SB_KV_HEADS, HEAD_DIM)
    mla_q = mla_q.reshape(B, T, MLA_HEADS, MLA_NOPE + MLA_ROPE)
    q_nope = mla_q[..., :MLA_NOPE]
    q_rope = _rope(mla_q[..., MLA_NOPE:], pos)
    ckv = _rmsnorm(ckv, g_kv)
    kr = _rope(kr[:, :, None, :], pos)[:, :, 0]
    mb_q = _rope(mb_q.reshape(B, T, MOBA_HEADS, HEAD_DIM), pos)
    mb_k = _rope(mb_k.reshape(B, T, MOBA_KV_HEADS, HEAD_DIM), pos)
    mb_v = mb_v.reshape(B, T, MOBA_KV_HEADS, HEAD_DIM)
    new_rows = (sb_k, sb_v, ckv, kr, mb_k, mb_v)
    if past is None:
        full = new_rows
    else:
        full = tuple(jnp.concatenate([p_, n_], axis=1) for p_, n_ in zip(past, new_rows))
    k_sb, v_sb, ckv_all, kr_all, k_mb, v_mb = full
    o_sb = _sweep(_sb_block, Q_BLOCK, pos, (sb_q,), (k_sb, v_sb)).reshape(B, T, -1)
    q_lat = jnp.einsum('bthn,rhn->bthr', q_nope, w_uk)
    o_lat = _sweep(_mla_block, Q_BLOCK, pos, (q_lat, q_rope), (ckv_all, kr_all))
    o_mla = jnp.einsum('bthr,rhd->bthd', o_lat, w_uv).reshape(B, T, -1)
    kbt, vbt, kmean = _moba_blocks(k_mb, v_mb)
    o_mb = _sweep(_moba_block, MOBA_Q_BLOCK, pos, (mb_q,), (kbt, vbt, kmean)).reshape(B, T, -1)
    g = jax.nn.sigmoid(gates.reshape(B, T, N_BRANCH, D_MODEL))
    merged = (g[:, :, 0] * (o_sb @ w_o_sb) + g[:, :, 1] * (o_mla @ w_o_mla)
              + g[:, :, 2] * (o_mb @ w_o_moba))
    return merged @ w_out, new_rows


def _layer(x, pos, past, g_ffn1, w_ffn1_gu, w_ffn1_down, g_mix, w_in, g_kv, w_uk, w_uv,
           w_o_sb, w_o_mla, w_o_moba, w_out, g_ffn2, w_ffn2_gu, w_ffn2_down):
    x = x + 0.5 * _swiglu(_rmsnorm(x, g_ffn1), w_ffn1_gu, w_ffn1_down)
    mix, rows = _mixer(_rmsnorm(x, g_mix), pos, past, w_in, g_kv, w_uk, w_uv,
                       w_o_sb, w_o_mla, w_o_moba, w_out)
    x = x + mix
    x = x + 0.5 * _swiglu(_rmsnorm(x, g_ffn2), w_ffn2_gu, w_ffn2_down)
    return x, rows


def _gather_pages(cache, layer, page_table):
    g = cache[layer, page_table]
    return g.reshape(page_table.shape[0], page_table.shape[1] * cache.shape[2], *cache.shape[3:])


def setup_inputs(seed: int = 0) -> dict:
    key = jax.random.key(seed)
    k = jax.random.split(key, 32)
    f32 = jnp.float32

    def nrm(i, shape, scale=1.0):
        return jax.random.normal(k[i], shape, f32) * scale

    def gain(i, shape):
        return 1.0 + 0.05 * jax.random.normal(k[i], shape, f32)

    n_pages = PAST_LEN // PAGE_SIZE
    n_used = DEC_BATCH * n_pages
    n_pool = n_used + max(1, n_used // 4)
    page_table = jax.random.permutation(k[0], n_pool)[:n_used].reshape(DEC_BATCH, n_pages).astype(jnp.int32)
    sb_shape = (DEPTH, n_pool, PAGE_SIZE, SB_KV_HEADS, HEAD_DIM)
    mb_shape = (DEPTH, n_pool, PAGE_SIZE, MOBA_KV_HEADS, HEAD_DIM)
    br = SB_HEADS * HEAD_DIM
    return {
        "x_prompt": nrm(1, (BATCH, SEQ, D_MODEL)),
        "x_sample": nrm(2, (DEC_BATCH, DEC_SEQ, D_MODEL)),
        "cache_sb_k": nrm(3, sb_shape),
        "cache_sb_v": nrm(4, sb_shape),
        "cache_mla_ckv": nrm(5, (DEPTH, n_pool, PAGE_SIZE, KV_LORA)),
        "cache_mla_krope": nrm(6, (DEPTH, n_pool, PAGE_SIZE, MLA_ROPE)),
        "cache_moba_k": nrm(7, mb_shape),
        "cache_moba_v": nrm(8, mb_shape),
        "page_table": page_table,
        "g_ffn1": gain(9, (DEPTH, D_MODEL)),
        "w_ffn1_gu": nrm(10, (DEPTH, D_MODEL, 2 * D_FF), D_MODEL ** -0.5),
        "w_ffn1_down": nrm(11, (DEPTH, D_FF, D_MODEL), D_FF ** -0.5),
        "g_mix": gain(12, (DEPTH, D_MODEL)),
        "w_in": nrm(13, (DEPTH, D_MODEL, D_IN), D_MODEL ** -0.5),
        "g_kv": gain(14, (DEPTH, KV_LORA)),
        "w_uk": nrm(15, (DEPTH, KV_LORA, MLA_HEADS, MLA_NOPE), KV_LORA ** -0.5),
        "w_uv": nrm(16, (DEPTH, KV_LORA, MLA_HEADS, MLA_V), KV_LORA ** -0.5),
        "w_o_sb": nrm(17, (DEPTH, br, D_MODEL), br ** -0.5),
        "w_o_mla": nrm(18, (DEPTH, MLA_HEADS * MLA_V, D_MODEL), (MLA_HEADS * MLA_V) ** -0.5),
        "w_o_moba": nrm(19, (DEPTH, MOBA_HEADS * HEAD_DIM, D_MODEL), (MOBA_HEADS * HEAD_DIM) ** -0.5),
        "w_out": nrm(20, (DEPTH, D_MODEL, D_MODEL), D_MODEL ** -0.5),
        "g_ffn2": gain(21, (DEPTH, D_MODEL)),
        "w_ffn2_gu": nrm(22, (DEPTH, D_MODEL, 2 * D_FF), D_MODEL ** -0.5),
        "w_ffn2_down": nrm(23, (DEPTH, D_FF, D_MODEL), D_FF ** -0.5),
        "g_final": gain(24, (D_MODEL,)),
    }


def reference(x_prompt, x_sample, cache_sb_k, cache_sb_v, cache_mla_ckv, cache_mla_krope,
              cache_moba_k, cache_moba_v, page_table, g_ffn1, w_ffn1_gu, w_ffn1_down, g_mix,
              w_in, g_kv, w_uk, w_uv, w_o_sb, w_o_mla, w_o_moba, w_out, g_ffn2, w_ffn2_gu,
              w_ffn2_down, g_final):
    past_len = page_table.shape[1] * PAGE_SIZE
    pos_p = jnp.arange(x_prompt.shape[1], dtype=jnp.int32)
    pos_s = past_len + jnp.arange(x_sample.shape[1], dtype=jnp.int32)
    caches = (cache_sb_k, cache_sb_v, cache_mla_ckv, cache_mla_krope, cache_moba_k, cache_moba_v)
    hp, hs = x_prompt, x_sample
    rows_p, rows_s = [], []
    for l in range(DEPTH):
        lw = (g_ffn1[l], w_ffn1_gu[l], w_ffn1_down[l], g_mix[l], w_in[l], g_kv[l], w_uk[l],
              w_uv[l], w_o_sb[l], w_o_mla[l], w_o_moba[l], w_out[l], g_ffn2[l], w_ffn2_gu[l],
              w_ffn2_down[l])
        hp, rp = _layer(hp, pos_p, None, *lw)
        past = tuple(_gather_pages(c, l, page_table) for c in caches)
        hs, rs = _layer(hs, pos_s, past, *lw)
        rows_p.append(rp)
        rows_s.append(rs)
    y_prompt = _rmsnorm(hp, g_final)
    y_sample = _rmsnorm(hs, g_final)
    sb_k_p, sb_v_p, ckv_p, kr_p, mb_k_p, mb_v_p = [jnp.stack(r) for r in zip(*rows_p)]
    sb_k_s, sb_v_s, ckv_s, kr_s, mb_k_s, mb_v_s = [jnp.stack(r) for r in zip(*rows_s)]
    return (y_prompt, y_sample, sb_k_p, sb_v_p, ckv_p, kr_p, mb_k_p, mb_v_p,
            sb_k_s, sb_v_s, ckv_s, kr_s, mb_k_s, mb_v_s)
```

```python
import functools
import math

import jax
import jax.numpy as jnp
import numpy as np
from jax import lax
from jax.experimental import pallas as pl
from jax.experimental.pallas import tpu as pltpu

F32 = jnp.float32
BF16 = jnp.bfloat16

D_MODEL = 1024
HEAD_DIM = 64
N_HEADS = 8
KV_HEADS = 2
GROUP = N_HEADS // KV_HEADS
MLA_NOPE = 64
MLA_ROPE = 32
KV_LORA = 128
MOBA_BLOCK = 256
MOBA_TOPK = 3
N_BRANCH = 3
D_FF = 2816
ROPE_THETA = 10000.0
EPS = 1e-6
PAGE = 128
LANES = 128
SLOT = N_HEADS * LANES

V7X_VMEM_BYTES = 64 * 1024 * 1024
VMEM_LIMIT = 52 * 1024 * 1024

NEG_INF = float("-inf")


def _cparams(sem):
    return pltpu.CompilerParams(dimension_semantics=sem, vmem_limit_bytes=VMEM_LIMIT)


def _dot(a, b):
    return jnp.dot(a, b, preferred_element_type=F32)


def _dot_nt(a, b):
    return lax.dot_general(a, b, (((1,), (1,)), ((), ())), preferred_element_type=F32)


def _split3(x):
    hi = x.astype(BF16)
    r = x - hi.astype(F32)
    mid = r.astype(BF16)
    lo = (r - mid.astype(F32)).astype(BF16)
    return hi, mid, lo


def _dot_f32(a, b, dot=_dot):
    a1, a2, a3 = _split3(a)
    b1, b2, b3 = _split3(b)
    return (dot(a1, b1) + (dot(a1, b2) + dot(a2, b1))
            + (dot(a1, b3) + dot(a2, b2) + dot(a3, b1)))


def _dot_nt_f32(a, b):
    return _dot_f32(a, b, _dot_nt)


def _rms(x, g):
    return x * lax.rsqrt(jnp.mean(x * x, axis=-1, keepdims=True) + EPS) * g


def _rope(x, cos, sin_signed, half):
    lane = lax.broadcasted_iota(jnp.int32, x.shape, 1)
    first = (lane % (2 * half)) < half
    rot = jnp.where(first, pltpu.roll(x, LANES - half, 1), pltpu.roll(x, half, 1))
    return x * cos + rot * sin_signed


def _log_sigmoid(z):
    return jnp.minimum(z, 0.0) - jnp.log1p(jnp.exp(-jnp.abs(z)))


def _const_spec(shape):
    return pl.BlockSpec(shape, lambda *_: (0,) * len(shape))


def _ffn_kernel(x_ref, g_ref, wg_ref, wu_ref, wd_ref, gf_ref, o_ref, h_ref, *, tf, final_norm):
    x = x_ref[...]
    xn = _rms(x, g_ref[...]).astype(BF16)
    for c in range(D_FF // tf):
        sl = slice(c * tf, (c + 1) * tf)
        a = _dot(xn, wg_ref[:, sl])
        b = _dot(xn, wu_ref[:, sl])
        h_ref[:, sl] = (a * jax.nn.sigmoid(a) * b).astype(BF16)
    y = x + 0.5 * _dot(h_ref[...], wd_ref[...])
    if final_norm:
        y = _rms(y, gf_ref[...])
    o_ref[...] = y


def _ffn(x, g, wg, wu, wd, g_final, tm, final_norm):
    m = x.shape[0]
    kern = functools.partial(_ffn_kernel, tf=256, final_norm=final_norm)
    return pl.pallas_call(
        kern,
        grid=(m // tm,),
        in_specs=[pl.BlockSpec((tm, D_MODEL), lambda i: (i, 0)),
                  _const_spec((1, D_MODEL)),
                  _const_spec((D_MODEL, D_FF)),
                  _const_spec((D_MODEL, D_FF)),
                  _const_spec((D_FF, D_MODEL)),
                  _const_spec((1, D_MODEL))],
        out_specs=pl.BlockSpec((tm, D_MODEL), lambda i: (i, 0)),
        out_shape=jax.ShapeDtypeStruct((m, D_MODEL), F32),
        scratch_shapes=[pltpu.VMEM((tm, D_FF), BF16)],
        compiler_params=_cparams(("arbitrary",)),
        name="ffn",
    )(x, g, wg, wu, wd, g_final)


_SEC = {}
_off = 0
for _name, _w in (("sbq", SLOT), ("sbk", LANES), ("sbv", LANES), ("qnope", N_HEADS * MLA_NOPE),
                  ("qrope", SLOT), ("ckv", KV_LORA), ("kr", LANES), ("mbq", SLOT), ("mbk", LANES),
                  ("mbv", LANES), ("gates", N_BRANCH * D_MODEL)):
    _SEC[_name] = (_off, _off + _w)
    _off += _w
W_ALL = _off


def _proj_kernel(x_ref, g_ref, w_ref, gkv_ref, wuk_ref, c64_ref, s64_ref, c32_ref, s32_ref,
                 sbq_ref, sbk_ref, sbv_ref, sbkb_ref, sbvb_ref, qlat_ref, qrope_ref, ckv_ref,
                 kr_ref, kcat_ref, mbq_ref, mbk_ref, mbv_ref, mbkb_ref, mbvb_ref, gates_ref):
    xn = _rms(x_ref[...], g_ref[...]).astype(BF16)
    c64, s64, c32, s32 = c64_ref[...], s64_ref[...], c32_ref[...], s32_ref[...]

    def sec(name, j=0, width=None):
        lo, hi = _SEC[name]
        if width is not None:
            lo, hi = lo + j * width, lo + (j + 1) * width
        return _dot(xn, w_ref[:, lo:hi])

    for h in range(N_HEADS):
        sl = slice(h * LANES, (h + 1) * LANES)
        sbq_ref[:, sl] = (sec("sbq", h, LANES) * (HEAD_DIM ** -0.5)).astype(BF16)
        qrope_ref[:, sl] = _rope(sec("qrope", h, LANES), c32, s32, MLA_ROPE // 2).astype(BF16)
        mbq_ref[:, sl] = _rope(sec("mbq", h, LANES), c64, s64, HEAD_DIM // 2)

    sbk = sec("sbk")
    sbk_ref[...] = sbk
    sbkb_ref[...] = sbk.astype(BF16)
    sbv = sec("sbv")
    sbv_ref[...] = sbv
    sbvb_ref[...] = sbv.astype(BF16)

    qnope = sec("qnope").astype(BF16)
    qlat_ref[...] = _dot(qnope, wuk_ref[...]).astype(BF16)

    ckv = _rms(sec("ckv"), gkv_ref[...])
    ckv_ref[...] = ckv
    kcat_ref[:, :KV_LORA] = ckv.astype(BF16)
    kr = _rope(sec("kr"), c32, s32, MLA_ROPE // 2)
    kr_ref[...] = kr[:, :MLA_ROPE]
    kcat_ref[:, KV_LORA:] = kr.astype(BF16)

    mbk = _rope(sec("mbk"), c64, s64, HEAD_DIM // 2)
    mbk_ref[...] = mbk
    mbkb_ref[...] = mbk.astype(BF16)
    mbv = sec("mbv")
    mbv_ref[...] = mbv
    mbvb_ref[...] = mbv.astype(BF16)

    for j in range(N_BRANCH * D_MODEL // 512):
        gates_ref[:, j * 512:(j + 1) * 512] = jax.nn.sigmoid(sec("gates", j, 512))


def _proj(x, g, w_all, gkv, wuk_bd, tabs, tm, seq_blocks):
    m = x.shape[0]
    row = lambda w: pl.BlockSpec((tm, w), lambda i: (i, 0))
    tab = pl.BlockSpec((tm, LANES), lambda i: (i % seq_blocks, 0))
    sd = jax.ShapeDtypeStruct
    outs = [(SLOT, BF16), (LANES, F32), (LANES, F32), (LANES, BF16), (LANES, BF16),
            (SLOT, BF16), (SLOT, BF16), (KV_LORA, F32), (MLA_ROPE, F32), (2 * LANES, BF16),
            (SLOT, F32), (LANES, F32), (LANES, F32), (LANES, BF16), (LANES, BF16),
            (N_BRANCH * D_MODEL, F32)]
    return pl.pallas_call(
        _proj_kernel,
        grid=(m // tm,),
        in_specs=[row(D_MODEL), _const_spec((1, D_MODEL)), _const_spec((D_MODEL, W_ALL)),
                  _const_spec((1, KV_LORA)), _const_spec((N_HEADS * MLA_NOPE, SLOT)),
                  tab, tab, tab, tab],
        out_specs=[row(w) for w, _ in outs],
        out_shape=[sd((m, w), dt) for w, dt in outs],
        compiler_params=_cparams(("arbitrary",)),
        name="proj",
    )(x, g, w_all, gkv, wuk_bd, *tabs)


def _stack_heads(ref, heads):
    return jnp.concatenate([ref[:, h * LANES:(h + 1) * LANES] for h in heads], axis=0)


def _sb_prompt_kernel(q_ref, k_ref, v_ref, u_ref, o_ref, *, tq):
    i = pl.program_id(1)
    rows = GROUP * tq
    u3 = u_ref[...]
    qs = [_stack_heads(q_ref, range(n * GROUP, (n + 1) * GROUP)) for n in range(KV_HEADS)]
    rr = lax.broadcasted_iota(jnp.int32, (rows, tq), 0) % tq
    cc = lax.broadcasted_iota(jnp.int32, (rows, tq), 1)
    strict = cc < rr

    def block(j, state, diag):
        kb = k_ref[pl.ds(pl.multiple_of(j * tq, tq), tq), :]
        vb = v_ref[pl.ds(pl.multiple_of(j * tq, tq), tq), :]
        out = []
        for n in range(KV_HEADS):
            carry, acc = state[n]
            z = _dot_nt(qs[n], kb)
            ls = _log_sigmoid(z)
            lk = ls - z
            if diag:
                lk = jnp.where(strict, lk, 0.0)
            hi, mid, lo = _split3(lk)
            after = _dot(jnp.concatenate([hi, mid, lo], axis=1), u3) + carry
            a = jnp.exp(ls + after)
            if diag:
                a = jnp.where(strict, a, 0.0)
            acc = acc + _dot(a.astype(BF16), vb)
            carry = carry + jnp.sum(lk, axis=-1, keepdims=True)
            out.append((carry, acc))
        return tuple(out)

    zero = (jnp.zeros((rows, 1), F32), jnp.zeros((rows, LANES), F32))
    state = block(i, (zero,) * KV_HEADS, True)
    state = lax.fori_loop(0, i, lambda t, s: block(i - 1 - t, s, False), state)
    for n in range(KV_HEADS):
        acc = state[n][1]
        for g in range(GROUP):
            h = n * GROUP + g
            o_ref[:, h * LANES:(h + 1) * LANES] = acc[g * tq:(g + 1) * tq].astype(BF16)


def _suffix_matrix(tk):
    u = (np.arange(tk)[:, None] > np.arange(tk)[None, :]).astype(np.float32)
    return jnp.asarray(np.concatenate([u, u, u], axis=0), dtype=BF16)


def _sb_prompt(q, k, v, batch, seq, tq=128):
    nq = seq // tq
    return pl.pallas_call(
        functools.partial(_sb_prompt_kernel, tq=tq),
        grid=(batch, nq),
        in_specs=[pl.BlockSpec((tq, SLOT), lambda b, i: (b * nq + i, 0)),
                  pl.BlockSpec((seq, LANES), lambda b, i: (b, 0)),
                  pl.BlockSpec((seq, LANES), lambda b, i: (b, 0)),
                  _const_spec((3 * tq, tq))],
        out_specs=pl.BlockSpec((tq, SLOT), lambda b, i: (b * nq + i, 0)),
        out_shape=jax.ShapeDtypeStruct((batch * seq, SLOT), BF16),
        compiler_params=_cparams(("arbitrary", "arbitrary")),
        name="sb_prompt",
    )(q, k, v, _suffix_matrix(tq))


def _mla_prompt_kernel(ql_ref, qr_ref, k_ref, o_ref, *, tq):
    i = pl.program_id(1)
    rows = N_HEADS * tq
    scale = (MLA_NOPE + MLA_ROPE) ** -0.5
    q = jnp.concatenate([_stack_heads(ql_ref, range(N_HEADS)),
                         _stack_heads(qr_ref, range(N_HEADS))], axis=1)
    rr = lax.broadcasted_iota(jnp.int32, (rows, tq), 0) % tq
    cc = lax.broadcasted_iota(jnp.int32, (rows, tq), 1)
    causal = cc <= rr

    def block(j, state, diag):
        m, l, acc = state
        kb = k_ref[pl.ds(pl.multiple_of(j * tq, tq), tq), :]
        s = _dot_nt(q, kb) * scale
        if diag:
            s = jnp.where(causal, s, NEG_INF)
        m_new = jnp.maximum(m, jnp.max(s, axis=-1, keepdims=True))
        alpha = jnp.exp(m - m_new)
        p = jnp.exp(s - m_new)
        l = l * alpha + jnp.sum(p, axis=-1, keepdims=True)
        acc = acc * alpha + _dot(p.astype(BF16), kb[:, :KV_LORA])
        return m_new, l, acc

    state = (jnp.full((rows, 1), NEG_INF, F32), jnp.zeros((rows, 1), F32),
             jnp.zeros((rows, KV_LORA), F32))
    state = block(i, state, True)
    state = lax.fori_loop(0, i, lambda t, s: block(t, s, False), state)
    _, l, acc = state
    o = acc / l
    for h in range(N_HEADS):
        o_ref[:, h * LANES:(h + 1) * LANES] = o[h * tq:(h + 1) * tq].astype(BF16)


def _mla_prompt(ql, qr, kcat, batch, seq, tq=128):
    nq = seq // tq
    qspec = pl.BlockSpec((tq, SLOT), lambda b, i: (b * nq + i, 0))
    return pl.pallas_call(
        functools.partial(_mla_prompt_kernel, tq=tq),
        grid=(batch, nq),
        in_specs=[qspec, qspec, pl.BlockSpec((seq, 2 * LANES), lambda b, i: (b, 0))],
        out_specs=qspec,
        out_shape=jax.ShapeDtypeStruct((batch * seq, SLOT), BF16),
        compiler_params=_cparams(("arbitrary", "arbitrary")),
        name="mla_prompt",
    )(ql, qr, kcat)


def _block_mean_kernel(k_ref, o_ref, *, nblk):
    for j in range(nblk):
        blk = k_ref[j * MOBA_BLOCK:(j + 1) * MOBA_BLOCK, :]
        o_ref[j:j + 1, :] = jnp.sum(blk, axis=0, keepdims=True) * (1.0 / MOBA_BLOCK)


def _block_mean(k):
    m = k.shape[0]
    nb = m // MOBA_BLOCK
    nblk = math.gcd(nb, 8)
    return pl.pallas_call(
        functools.partial(_block_mean_kernel, nblk=nblk),
        grid=(nb // nblk,),
        in_specs=[pl.BlockSpec((nblk * MOBA_BLOCK, LANES), lambda i: (i, 0))],
        out_specs=pl.BlockSpec((nblk, LANES), lambda i: (i, 0)),
        out_shape=jax.ShapeDtypeStruct((nb, LANES), F32),
        compiler_params=_cparams(("arbitrary",)),
        name="block_mean",
    )(k)


def _top_blocks(gate, n_valid):
    lane = lax.broadcasted_iota(jnp.int32, gate.shape, 1)
    lane_f = lane.astype(F32)
    valid = lane < n_valid
    cur = jnp.where(valid, gate, NEG_INF)
    sel = jnp.zeros(gate.shape, F32)
    picks = []
    for _ in range(MOBA_TOPK):
        mx = jnp.max(cur, axis=-1, keepdims=True)
        idx = jnp.min(jnp.where(cur == mx, lane_f, 1e9), axis=-1, keepdims=True)
        pick = lane_f == idx
        sel = jnp.maximum(sel, jnp.where(pick & valid, 1.0, 0.0))
        cur = jnp.where(pick, NEG_INF, cur)
        picks.append(idx)
    return sel, picks


def _moba_prompt_kernel(q_ref, k_ref, v_ref, km_ref, o_ref, *, nb):
    tq = MOBA_BLOCK
    i = pl.program_id(1)
    rows = GROUP * tq
    scale = HEAD_DIM ** -0.5
    km = km_ref[...]
    rr = lax.broadcasted_iota(jnp.int32, (rows, tq), 0) % tq
    cc = lax.broadcasted_iota(jnp.int32, (rows, tq), 1)
    causal = cc <= rr
    kown = k_ref[pl.ds(pl.multiple_of(i * tq, tq), tq), :]
    vown = v_ref[pl.ds(pl.multiple_of(i * tq, tq), tq), :]
    for n in range(KV_HEADS):
        qf = _stack_heads(q_ref, range(n * GROUP, (n + 1) * GROUP))
        qb = qf.astype(BF16)
        sel, _ = _top_blocks(_dot_nt_f32(qf, km), i)
        lane = lax.broadcasted_iota(jnp.int32, sel.shape, 1)

        s = jnp.where(causal, _dot_nt(qb, kown) * scale, NEG_INF)
        m = jnp.max(s, axis=-1, keepdims=True)
        p = jnp.exp(s - m)
        l = jnp.sum(p, axis=-1, keepdims=True)
        acc = _dot(p.astype(BF16), vown)

        def past(j, state):
            m, l, acc = state
            kb = k_ref[pl.ds(pl.multiple_of(j * tq, tq), tq), :]
            vb = v_ref[pl.ds(pl.multiple_of(j * tq, tq), tq), :]
            chosen = jnp.max(jnp.where(lane == j, sel, 0.0), axis=-1, keepdims=True) > 0.0
            s = jnp.where(chosen, _dot_nt(qb, kb) * scale, NEG_INF)
            m_new = jnp.maximum(m, jnp.max(s, axis=-1, keepdims=True))
            alpha = jnp.exp(m - m_new)
            p = jnp.exp(s - m_new)
            l = l * alpha + jnp.sum(p, axis=-1, keepdims=True)
            acc = acc * alpha + _dot(p.astype(BF16), vb)
            return m_new, l, acc

        m, l, acc = lax.fori_loop(0, i, past, (m, l, acc))
        o = acc / l
        for g in range(GROUP):
            h = n * GROUP + g
            o_ref[:, h * LANES:(h + 1) * LANES] = o[g * tq:(g + 1) * tq].astype(BF16)


def _moba_prompt(q, k, v, kmean, batch, seq):
    nb = seq // MOBA_BLOCK
    qspec = pl.BlockSpec((MOBA_BLOCK, SLOT), lambda b, i: (b * nb + i, 0))
    kv = pl.BlockSpec((seq, LANES), lambda b, i: (b, 0))
    return pl.pallas_call(
        functools.partial(_moba_prompt_kernel, nb=nb),
        grid=(batch, nb),
        in_specs=[qspec, kv, kv, pl.BlockSpec((nb, LANES), lambda b, i: (b, 0))],
        out_specs=qspec,
        out_shape=jax.ShapeDtypeStruct((batch * seq, SLOT), BF16),
        compiler_params=_cparams(("arbitrary", "arbitrary")),
        name="moba_prompt",
    )(q, k, v, kmean)


class _Pager:
    def __init__(self, pt_ref, layer, caches, bufs, sems, pages_per_chunk, chunk_of_step):
        self.pt, self.layer, self.caches, self.bufs, self.sems = pt_ref, layer, caches, bufs, sems
        self.p, self.chunk_of_step = pages_per_chunk, chunk_of_step

    def _copies(self, b, chunk, slot):
        out = []
        for cache, buf, sem in zip(self.caches, self.bufs, self.sems):
            for p in range(self.p):
                page = self.pt[b, chunk * self.p + p]
                out.append(pltpu.make_async_copy(cache.at[self.layer, page], buf.at[slot, p],
                                                 sem.at[slot]))
        return out

    def step(self):
        b, c = pl.program_id(0), pl.program_id(1)
        nb, nc = pl.num_programs(0), pl.num_programs(1)
        s = b * nc + c
        slot = s % 2

        @pl.when(s == 0)
        def _():
            for cp in self._copies(b, self.chunk_of_step(c), slot):
                cp.start()

        @pl.when(s + 1 < nb * nc)
        def _():
            wrap = c + 1 == nc
            b2 = jnp.where(wrap, b + 1, b)
            c2 = jnp.where(wrap, 0, c + 1)
            for cp in self._copies(b2, self.chunk_of_step(c2), 1 - slot):
                cp.start()

        for cp in self._copies(b, self.chunk_of_step(c), slot):
            cp.wait()
        return slot


def _sb_sample_kernel(pt_ref, q_ref, u_ref, k_hbm, v_hbm, o_ref, kbuf, vbuf, ksem, vsem, carry_ref,
                      acc_ref, *, layer, ppc):
    c, nc = pl.program_id(1), pl.num_programs(1)
    pager = _Pager(pt_ref, layer, (k_hbm, v_hbm), (kbuf, vbuf), (ksem, vsem), ppc,
                   lambda cc: nc - 1 - cc)
    slot = pager.step()

    @pl.when(c == 0)
    def _():
        carry_ref[...] = jnp.zeros_like(carry_ref)
        acc_ref[...] = jnp.zeros_like(acc_ref)

    q = q_ref[0]
    z = jnp.concatenate([_dot(q, kbuf[slot, p].astype(BF16)) for p in range(ppc)], axis=0)
    ls = _log_sigmoid(z)
    lk = ls - z
    hi, mid, lo = _split3(lk)
    within = _dot(jnp.concatenate([hi, mid, lo], axis=1), u_ref[...])
    tot = jnp.sum(lk, axis=-1, keepdims=True)
    carry = carry_ref[...]
    acc = acc_ref[...]
    for p in reversed(range(ppc)):
        r = slice(p * N_HEADS, (p + 1) * N_HEADS)
        a = jnp.exp(ls[r] + within[r] + carry)
        acc = acc + _dot_nt(a.astype(BF16), vbuf[slot, p].astype(BF16))
        carry = carry + tot[r]
    carry_ref[...] = carry
    acc_ref[...] = acc

    @pl.when(c == nc - 1)
    def _():
        o_ref[0] = acc.astype(BF16)


def _paged_call(kernel, name, pt, operands, operand_specs, caches, out_dtype, scratch, ppc):
    nseq, npages = pt.shape
    bufs = [pltpu.VMEM((2, ppc) + c.shape[2:], c.dtype) for c in caches]
    sems = [pltpu.SemaphoreType.DMA((2,)) for _ in caches]
    return pl.pallas_call(
        kernel,
        grid_spec=pltpu.PrefetchScalarGridSpec(
            num_scalar_prefetch=1,
            grid=(nseq, npages // ppc),
            in_specs=operand_specs + [pl.BlockSpec(memory_space=pl.ANY) for _ in caches],
            out_specs=pl.BlockSpec((1, N_HEADS, LANES), lambda b, c, pt: (b, 0, 0)),
            scratch_shapes=bufs + sems + scratch),
        out_shape=jax.ShapeDtypeStruct((nseq, N_HEADS, LANES), out_dtype),
        compiler_params=_cparams(("arbitrary", "arbitrary")),
        name=name,
    )(pt, *operands, *caches)


def _head_spec():
    return pl.BlockSpec((1, N_HEADS, LANES), lambda b, c, pt: (b, 0, 0))


def _sb_sample(pt, q, cache_k, cache_v, layer, ppc=16):
    nseq = pt.shape[0]
    u3 = _suffix_matrix(PAGE)
    scratch = [pltpu.VMEM((N_HEADS, 1), F32), pltpu.VMEM((N_HEADS, LANES), F32)]
    out = _paged_call(
        functools.partial(_sb_sample_kernel, layer=layer, ppc=ppc), "sb_sample", pt,
        [q.reshape(nseq, N_HEADS, LANES), u3],
        [_head_spec(), pl.BlockSpec((3 * PAGE, PAGE), lambda b, c, pt: (0, 0))],
        [cache_k, cache_v], BF16, scratch, ppc)
    return out.reshape(nseq, SLOT)


def _mla_sample_kernel(pt_ref, ql_ref, qr_ref, cnew_ref, rnew_ref, c_hbm, r_hbm, o_ref, cbuf, rbuf,
                       csem, rsem, m_ref, l_ref, acc_ref, *, layer, ppc):
    c, nc = pl.program_id(1), pl.num_programs(1)
    pager = _Pager(pt_ref, layer, (c_hbm, r_hbm), (cbuf, rbuf), (csem, rsem), ppc, lambda cc: cc)
    slot = pager.step()
    scale = (MLA_NOPE + MLA_ROPE) ** -0.5

    @pl.when(c == 0)
    def _():
        m_ref[...] = jnp.full_like(m_ref, NEG_INF)
        l_ref[...] = jnp.zeros_like(l_ref)
        acc_ref[...] = jnp.zeros_like(acc_ref)

    ql = ql_ref[0]
    qr = qr_ref[0][:, :MLA_ROPE]
    pages = [cbuf[slot, p].astype(BF16) for p in range(ppc)]
    s = [(_dot_nt(ql, pages[p]) + _dot(qr, rbuf[slot, p].astype(BF16))) * scale
         for p in range(ppc)]
    m_old = m_ref[...]
    m_new = jnp.maximum(m_old, jnp.max(functools.reduce(jnp.maximum, s), axis=-1, keepdims=True))
    alpha = jnp.exp(m_old - m_new)
    psum = jnp.zeros((N_HEADS, LANES), F32)
    acc = acc_ref[...] * alpha
    for p in range(ppc):
        pr = jnp.exp(s[p] - m_new)
        psum = psum + pr
        acc = acc + _dot(pr.astype(BF16), pages[p])
    l = l_ref[...] * alpha + jnp.sum(psum, axis=-1, keepdims=True)
    m_ref[...] = m_new
    l_ref[...] = l
    acc_ref[...] = acc

    @pl.when(c == nc - 1)
    def _():
        cnew = cnew_ref[0]
        s_new = (jnp.sum(ql.astype(F32) * cnew.astype(BF16).astype(F32), axis=-1, keepdims=True)
                 + jnp.sum(qr.astype(F32) * rnew_ref[0].astype(BF16).astype(F32), axis=-1,
                           keepdims=True)) * scale
        m_fin = jnp.maximum(m_new, s_new)
        a2 = jnp.exp(m_new - m_fin)
        p_new = jnp.exp(s_new - m_fin)
        l_fin = l * a2 + p_new
        o = (acc * a2 + p_new.astype(BF16).astype(F32) * cnew.astype(BF16).astype(F32)) / l_fin
        o_ref[0] = o.astype(BF16)


def _mla_sample(pt, ql, qr, ckv_new, kr_new, cache_c, cache_r, layer, ppc=16):
    nseq = pt.shape[0]
    scratch = [pltpu.VMEM((N_HEADS, 1), F32), pltpu.VMEM((N_HEADS, 1), F32),
               pltpu.VMEM((N_HEADS, LANES), F32)]
    out = _paged_call(
        functools.partial(_mla_sample_kernel, layer=layer, ppc=ppc), "mla_sample", pt,
        [ql.reshape(nseq, N_HEADS, LANES), qr.reshape(nseq, N_HEADS, LANES),
         ckv_new.reshape(nseq, 1, KV_LORA), kr_new.reshape(nseq, 1, MLA_ROPE)],
        [_head_spec(), _head_spec(),
         pl.BlockSpec((1, 1, KV_LORA), lambda b, c, pt: (b, 0, 0)),
         pl.BlockSpec((1, 1, MLA_ROPE), lambda b, c, pt: (b, 0, 0))],
        [cache_c, cache_r], BF16, scratch, ppc)
    return out.reshape(nseq, SLOT)


def _moba_select_kernel(pt_ref, q_ref, k_hbm, o_ref, kbuf, ksem, km_ref, *, layer, ppc, n_blocks):
    c, nc = pl.program_id(1), pl.num_programs(1)
    pager = _Pager(pt_ref, layer, (k_hbm,), (kbuf,), (ksem,), ppc, lambda cc: cc)
    slot = pager.step()
    ppb = MOBA_BLOCK // PAGE
    bpc = ppc // ppb
    lane_blk = lax.broadcasted_iota(jnp.int32, (LANES, LANES), 1)

    @pl.when(c == 0)
    def _():
        km_ref[...] = jnp.zeros_like(km_ref)

    km = km_ref[...]
    for j in range(bpc):
        tot = kbuf[slot, j * ppb]
        for t in range(1, ppb):
            tot = tot + kbuf[slot, j * ppb + t]
        mean = jnp.sum(tot, axis=1, keepdims=True) * (1.0 / MOBA_BLOCK)
        km = jnp.where(lane_blk == c * bpc + j, mean, km)
    km_ref[...] = km

    @pl.when(c == nc - 1)
    def _():
        gate = _dot_f32(q_ref[0], km)
        _, picks = _top_blocks(gate, n_blocks)
        lane = lax.broadcasted_iota(jnp.int32, (N_HEADS, LANES), 1)
        out = jnp.zeros((N_HEADS, LANES), F32)
        for r, idx in enumerate(picks):
            out = jnp.where(lane == r, idx, out)
        o_ref[0] = out.astype(jnp.int32)


def _moba_select(pt, q, cache_k, layer, ppc=16):
    nseq, npages = pt.shape
    n_blocks = npages * PAGE // MOBA_BLOCK
    assert MOBA_TOPK <= n_blocks <= LANES
    scratch = [pltpu.VMEM((LANES, LANES), F32)]
    return _paged_call(
        functools.partial(_moba_select_kernel, layer=layer, ppc=ppc, n_blocks=n_blocks),
        "moba_select", pt,
        [q.reshape(nseq, N_HEADS, LANES)], [_head_spec()], [cache_k], jnp.int32, scratch, ppc)


def _moba_sample_kernel(pg_ref, q_ref, knew_ref, vnew_ref, k_hbm, v_hbm, o_ref, kbuf, vbuf, ksem,
                        vsem, *, layer, npg):
    b, nb = pl.program_id(0), pl.num_programs(0)
    slot = b % 2
    per_head = npg // N_HEADS
    scale = HEAD_DIM ** -0.5

    def copies(bb, sl):
        out = []
        for cache, buf, sem in ((k_hbm, kbuf, ksem), (v_hbm, vbuf, vsem)):
            for p in range(npg):
                out.append(pltpu.make_async_copy(cache.at[layer, pg_ref[bb, p]], buf.at[sl, p],
                                                 sem.at[sl]))
        return out

    @pl.when(b == 0)
    def _():
        for cp in copies(b, slot):
            cp.start()

    @pl.when(b + 1 < nb)
    def _():
        for cp in copies(b + 1, 1 - slot):
            cp.start()

    for cp in copies(b, slot):
        cp.wait()

    qf = q_ref[0]
    qb = qf.astype(BF16)
    row = lax.broadcasted_iota(jnp.int32, (N_HEADS, LANES), 0)
    s = []
    for p in range(npg):
        sp = _dot(qb, kbuf[slot, p].astype(BF16)) * scale
        s.append(jnp.where(row == p // per_head, sp, NEG_INF))
    knew = knew_ref[0].astype(BF16).astype(F32)
    s_new = jnp.sum(qb.astype(F32) * knew, axis=-1, keepdims=True) * scale
    m = jnp.maximum(jnp.max(functools.reduce(jnp.maximum, s), axis=-1, keepdims=True), s_new)
    p_new = jnp.exp(s_new - m)
    psum = jnp.zeros((N_HEADS, LANES), F32)
    acc = p_new.astype(BF16).astype(F32) * vnew_ref[0].astype(BF16).astype(F32)
    for p in range(npg):
        pr = jnp.exp(s[p] - m)
        psum = psum + pr
        acc = acc + _dot_nt(pr.astype(BF16), vbuf[slot, p].astype(BF16))
    l = jnp.sum(psum, axis=-1, keepdims=True) + p_new
    o_ref[0] = (acc / l).astype(BF16)


def _moba_sample(pages, q, k_new, v_new, cache_k, cache_v, layer):
    nseq, npg = pages.shape
    head = pl.BlockSpec((1, N_HEADS, LANES), lambda b, pg: (b, 0, 0))
    new = pl.BlockSpec((1, 1, LANES), lambda b, pg: (b, 0, 0))
    out = pl.pallas_call(
        functools.partial(_moba_sample_kernel, layer=layer, npg=npg),
        grid_spec=pltpu.PrefetchScalarGridSpec(
            num_scalar_prefetch=1,
            grid=(nseq,),
            in_specs=[head, new, new, pl.BlockSpec(memory_space=pl.ANY),
                      pl.BlockSpec(memory_space=pl.ANY)],
            out_specs=head,
            scratch_shapes=[pltpu.VMEM((2, npg, PAGE, LANES), F32),
                            pltpu.VMEM((2, npg, PAGE, LANES), F32),
                            pltpu.SemaphoreType.DMA((2,)), pltpu.SemaphoreType.DMA((2,))]),
        out_shape=jax.ShapeDtypeStruct((nseq, N_HEADS, LANES), BF16),
        compiler_params=_cparams(("arbitrary",)),
        name="moba_sample",
    )(pages, q.reshape(nseq, N_HEADS, LANES), k_new.reshape(nseq, 1, LANES),
      v_new.reshape(nseq, 1, LANES), cache_k, cache_v)
    return out.reshape(nseq, SLOT)


def _merge_kernel(x_ref, osb_ref, olat_ref, omb_ref, g_ref, wsb_ref, wuv_ref, wmla_ref, wmb_ref,
                  wout_ref, o_ref):
    d = D_MODEL
    a = _dot(osb_ref[...], wsb_ref[...])
    b = _dot(_dot(olat_ref[...], wuv_ref[...]).astype(BF16), wmla_ref[...])
    c = _dot(omb_ref[...], wmb_ref[...])
    merged = g_ref[:, :d] * a + g_ref[:, d:2 * d] * b + g_ref[:, 2 * d:] * c
    o_ref[...] = x_ref[...] + _dot(merged.astype(BF16), wout_ref[...])


def _merge(x, osb, olat, omb, gates, wsb, wuv, wmla, wmb, wout, tm):
    m = x.shape[0]
    row = lambda w: pl.BlockSpec((tm, w), lambda i: (i, 0))
    return pl.pallas_call(
        _merge_kernel,
        grid=(m // tm,),
        in_specs=[row(D_MODEL), row(SLOT), row(SLOT), row(SLOT), row(N_BRANCH * D_MODEL),
                  _const_spec(wsb.shape), _const_spec(wuv.shape), _const_spec(wmla.shape),
                  _const_spec(wmb.shape), _const_spec(wout.shape)],
        out_specs=row(D_MODEL),
        out_shape=jax.ShapeDtypeStruct((m, D_MODEL), F32),
        compiler_params=_cparams(("arbitrary",)),
        name="merge",
    )(x, osb, olat, omb, gates, wsb, wuv, wmla, wmb, wout)


def _pad_heads(w, group_of_head=True):
    lead = w.shape[:-1]
    w = w.reshape(*lead, N_HEADS, 1, HEAD_DIM)
    half = (jnp.arange(N_HEADS)[:, None] // GROUP == jnp.arange(KV_HEADS)[None, :]).astype(w.dtype)
    return (w * half[:, :, None]).reshape(*lead, SLOT)


def _layer_weights(l, w_in, w_uk, w_uv, w_o_sb, w_o_mla, w_o_moba, w_out):
    w = w_in[l]
    d = D_MODEL
    o = np.cumsum([0, 512, 128, 128, 768, 128, 32, 512, 128, 128, 3072])
    sbq, sbk, sbv, mq, ckv, kr, mbq, mbk, mbv, gates = [w[:, o[i]:o[i + 1]] for i in range(10)]
    mq = mq.reshape(d, N_HEADS, MLA_NOPE + MLA_ROPE)
    qnope = mq[:, :, :MLA_NOPE].reshape(d, N_HEADS * MLA_NOPE)
    qrope = jnp.pad(mq[:, :, MLA_NOPE:], ((0, 0), (0, 0), (0, LANES - MLA_ROPE))).reshape(d, SLOT)
    kr = jnp.pad(kr, ((0, 0), (0, LANES - MLA_ROPE)))
    w_all = jnp.concatenate([_pad_heads(sbq), sbk, sbv, qnope, qrope, ckv, kr, _pad_heads(mbq),
                             mbk, mbv, gates], axis=1).astype(BF16)
    eye = jnp.eye(N_HEADS, dtype=F32)
    wuk_bd = (w_uk[l].transpose(1, 2, 0)[:, :, None, :] * eye[:, None, :, None]).reshape(
        N_HEADS * MLA_NOPE, SLOT).astype(BF16)
    wuv_bd = (w_uv[l].transpose(1, 0, 2)[:, :, None, :] * eye[:, None, :, None]).reshape(
        SLOT, N_HEADS * HEAD_DIM).astype(BF16)
    wsb = _pad_heads(w_o_sb[l].T).T.astype(BF16)
    wmb = _pad_heads(w_o_moba[l].T).T.astype(BF16)
    return dict(w_all=w_all, wuk=wuk_bd, wuv=wuv_bd, wsb=wsb, wmla=w_o_mla[l].astype(BF16), wmb=wmb,
                wout=w_out[l].astype(BF16))


def _rope_tables(pos):
    def one(dim):
        half = dim // 2
        inv = jnp.exp(-math.log(ROPE_THETA) * 2.0 * jnp.arange(half, dtype=F32) / dim)
        ang = pos.astype(F32)[:, None] * inv[None, :]
        cos, sin = jnp.cos(ang), jnp.sin(ang)
        reps = LANES // dim
        return (jnp.tile(jnp.concatenate([cos, cos], axis=1), (1, reps)),
                jnp.tile(jnp.concatenate([-sin, sin], axis=1), (1, reps)))
    c64, s64 = one(HEAD_DIM)
    c32, s32 = one(MLA_ROPE)
    return c64, s64, c32, s32


def kernel(x_prompt, x_sample, cache_sb_k, cache_sb_v, cache_mla_ckv, cache_mla_krope, cache_moba_k, cache_moba_v, page_table, g_ffn1, w_ffn1_gu, w_ffn1_down, g_mix, w_in, g_kv, w_uk, w_uv, w_o_sb, w_o_mla, w_o_moba, w_out, g_ffn2, w_ffn2_gu, w_ffn2_down, g_final):
    batch, seq, d = x_prompt.shape
    nseq, dec_seq, _ = x_sample.shape
    assert dec_seq == 1 and d == D_MODEL
    depth = w_in.shape[0]
    n_pool = cache_sb_k.shape[1]
    npages = page_table.shape[1]
    past_len = npages * PAGE
    assert past_len % MOBA_BLOCK == 0 and seq % MOBA_BLOCK == 0
    mp, ms = batch * seq, nseq
    tm_p, tm_s = 512, ms

    tabs_p = _rope_tables(jnp.arange(seq, dtype=jnp.int32))
    tabs_s = _rope_tables(jnp.full((ms,), past_len, dtype=jnp.int32))
    feat_major = lambda c: jnp.transpose(c, (0, 1, 3, 4, 2)).reshape(depth, n_pool, LANES, PAGE)
    csb_k, csb_v = feat_major(cache_sb_k), feat_major(cache_sb_v)
    cmb_k, cmb_v = feat_major(cache_moba_k), feat_major(cache_moba_v)
    ckr = jnp.transpose(cache_mla_krope, (0, 1, 3, 2))
    gf = g_final.reshape(1, d)

    hp = x_prompt.reshape(mp, d)
    hs = x_sample.reshape(ms, d)
    rows_p, rows_s = [], []
    for l in range(depth):
        lw = _layer_weights(l, w_in, w_uk, w_uv, w_o_sb, w_o_mla, w_o_moba, w_out)
        f1 = (g_ffn1[l].reshape(1, d), w_ffn1_gu[l][:, :D_FF].astype(BF16),
              w_ffn1_gu[l][:, D_FF:].astype(BF16), w_ffn1_down[l].astype(BF16), gf)
        f2 = (g_ffn2[l].reshape(1, d), w_ffn2_gu[l][:, :D_FF].astype(BF16),
              w_ffn2_gu[l][:, D_FF:].astype(BF16), w_ffn2_down[l].astype(BF16), gf)
        last = l == depth - 1
        gm, gkv = g_mix[l].reshape(1, d), g_kv[l].reshape(1, KV_LORA)
        merge_w = (lw["wsb"], lw["wuv"], lw["wmla"], lw["wmb"], lw["wout"])

        hp = _ffn(hp, *f1, tm_p, False)
        (sbq, sbk, sbv, sbkb, sbvb, qlat, qrope, ckv, kr, kcat, mbq, mbk, mbv, mbkb, mbvb,
         gates) = _proj(hp, gm, lw["w_all"], gkv, lw["wuk"], tabs_p, tm_p, seq // tm_p)
        osb = _sb_prompt(sbq, sbkb, sbvb, batch, seq)
        olat = _mla_prompt(qlat, qrope, kcat, batch, seq)
        kmean = _block_mean(mbk)
        omb = _moba_prompt(mbq, mbkb, mbvb, kmean, batch, seq)
        hp = _merge(hp, osb, olat, omb, gates, *merge_w, tm_p)
        hp = _ffn(hp, *f2, tm_p, last)
        rows_p.append((sbk.reshape(batch, seq, KV_HEADS, HEAD_DIM),
                       sbv.reshape(batch, seq, KV_HEADS, HEAD_DIM),
                       ckv.reshape(batch, seq, KV_LORA), kr.reshape(batch, seq, MLA_ROPE),
                       mbk.reshape(batch, seq, KV_HEADS, HEAD_DIM),
                       mbv.reshape(batch, seq, KV_HEADS, HEAD_DIM)))

        hs = _ffn(hs, *f1, tm_s, False)
        (sbq, sbk, sbv, _, _, qlat, qrope, ckv, kr, _, mbq, mbk, mbv, _, _,
         gates) = _proj(hs, gm, lw["w_all"], gkv, lw["wuk"], tabs_s, tm_s, 1)
        osb = _sb_sample(page_table, sbq, csb_k, csb_v, l)
        olat = _mla_sample(page_table, qlat, qrope, ckv, kr, cache_mla_ckv, ckr, l)
        sel = _moba_select(page_table, mbq, cmb_k, l)[:, :, :MOBA_TOPK]
        ppb = MOBA_BLOCK // PAGE
        page_idx = (sel[..., None] * ppb + jnp.arange(ppb, dtype=jnp.int32)).reshape(nseq, -1)
        pages = jnp.take_along_axis(page_table, page_idx, axis=1)
        omb = _moba_sample(pages, mbq, mbk, mbv, cmb_k, cmb_v, l)
        hs = _merge(hs, osb, olat, omb, gates, *merge_w, tm_s)
        hs = _ffn(hs, *f2, tm_s, last)
        rows_s.append((sbk.reshape(nseq, 1, KV_HEADS, HEAD_DIM),
                       sbv.reshape(nseq, 1, KV_HEADS, HEAD_DIM),
                       ckv.reshape(nseq, 1, KV_LORA), kr.reshape(nseq, 1, MLA_ROPE),
                       mbk.reshape(nseq, 1, KV_HEADS, HEAD_DIM),
                       mbv.reshape(nseq, 1, KV_HEADS, HEAD_DIM)))

    y_prompt = hp.reshape(batch, seq, d)
    y_sample = hs.reshape(nseq, 1, d)
    stacked_p = [jnp.stack(r) for r in zip(*rows_p)]
    stacked_s = [jnp.stack(r) for r in zip(*rows_s)]
    return (y_prompt, y_sample, *stacked_p, *stacked_s)
```

```python
import functools
import math

import jax
import jax.numpy as jnp
import numpy as np
from jax import lax
from jax.experimental import pallas as pl
from jax.experimental.pallas import tpu as pltpu

F32 = jnp.float32
BF16 = jnp.bfloat16

D_MODEL = 1024
HEAD_DIM = 64
N_HEADS = 8
KV_HEADS = 2
GROUP = N_HEADS // KV_HEADS
MLA_NOPE = 64
MLA_ROPE = 32
KV_LORA = 128
MOBA_BLOCK = 256
MOBA_TOPK = 3
N_BRANCH = 3
D_FF = 2816
ROPE_THETA = 10000.0
EPS = 1e-6
PAGE = 128
LANES = 128
SLOT = N_HEADS * LANES

V7X_VMEM_BYTES = 64 * 1024 * 1024
VMEM_LIMIT = 52 * 1024 * 1024

NEG_INF = float("-inf")
LOG2E = math.log2(math.e)
SB_NEGLIGIBLE = -110.0
MASK_BIAS = -1e30


def _cparams(sem):
    return pltpu.CompilerParams(dimension_semantics=sem, vmem_limit_bytes=VMEM_LIMIT)


def _dot(a, b):
    return jnp.dot(a, b, preferred_element_type=F32)


def _dot_nt(a, b):
    return lax.dot_general(a, b, (((1,), (1,)), ((), ())), preferred_element_type=F32)


def _split3(x):
    hi = x.astype(BF16)
    r = x - hi.astype(F32)
    mid = r.astype(BF16)
    lo = (r - mid.astype(F32)).astype(BF16)
    return hi, mid, lo


def _dot_f32(a, b, dot=_dot):
    a1, a2, a3 = _split3(a)
    b1, b2, b3 = _split3(b)
    return (dot(a1, b1) + (dot(a1, b2) + dot(a2, b1))
            + (dot(a1, b3) + dot(a2, b2) + dot(a3, b1)))


def _dot_nt_f32(a, b):
    return _dot_f32(a, b, _dot_nt)


def _rms(x, g):
    return x * lax.rsqrt(jnp.mean(x * x, axis=-1, keepdims=True) + EPS) * g


def _rope(x, cos, sin_signed, half):
    lane = lax.broadcasted_iota(jnp.int32, x.shape, 1)
    first = (lane % (2 * half)) < half
    rot = jnp.where(first, pltpu.roll(x, LANES - half, 1), pltpu.roll(x, half, 1))
    return x * cos + rot * sin_signed


def _const_spec(shape):
    return pl.BlockSpec(shape, lambda *_: (0,) * len(shape))


def _ffn_kernel(x_ref, g_ref, wg_ref, wu_ref, wd_ref, gf_ref, o_ref, h_ref, *, tf, final_norm):
    x = x_ref[...]
    xn = _rms(x, g_ref[...]).astype(BF16)
    for c in range(D_FF // tf):
        sl = slice(c * tf, (c + 1) * tf)
        a = _dot(xn, wg_ref[:, sl])
        b = _dot(xn, wu_ref[:, sl])
        h_ref[:, sl] = (a * jax.nn.sigmoid(a) * b).astype(BF16)
    y = x + 0.5 * _dot(h_ref[...], wd_ref[...])
    if final_norm:
        y = _rms(y, gf_ref[...])
    o_ref[...] = y


def _ffn(x, g, wg, wu, wd, g_final, tm, final_norm):
    m = x.shape[0]
    kern = functools.partial(_ffn_kernel, tf=256, final_norm=final_norm)
    return pl.pallas_call(
        kern,
        grid=(m // tm,),
        in_specs=[pl.BlockSpec((tm, D_MODEL), lambda i: (i, 0)),
                  _const_spec((1, D_MODEL)),
                  _const_spec((D_MODEL, D_FF)),
                  _const_spec((D_MODEL, D_FF)),
                  _const_spec((D_FF, D_MODEL)),
                  _const_spec((1, D_MODEL))],
        out_specs=pl.BlockSpec((tm, D_MODEL), lambda i: (i, 0)),
        out_shape=jax.ShapeDtypeStruct((m, D_MODEL), F32),
        scratch_shapes=[pltpu.VMEM((tm, D_FF), BF16)],
        compiler_params=_cparams(("arbitrary",)),
        name="ffn",
    )(x, g, wg, wu, wd, g_final)


_SEC = {}
_off = 0
for _name, _w in (("sbq", SLOT), ("sbk", LANES), ("sbv", LANES), ("qnope", N_HEADS * MLA_NOPE),
                  ("qrope", SLOT), ("ckv", KV_LORA), ("kr", LANES), ("mbq", SLOT), ("mbk", LANES),
                  ("mbv", LANES), ("gates", N_BRANCH * D_MODEL)):
    _SEC[_name] = (_off, _off + _w)
    _off += _w
W_ALL = _off


def _proj_kernel(x_ref, g_ref, w_ref, gkv_ref, wuk_ref, c64_ref, s64_ref, c32_ref, s32_ref,
                 sbq_ref, sbk_ref, sbv_ref, sbkb_ref, sbvb_ref, qlat_ref, qrope_ref, ckv_ref,
                 kr_ref, kcat_ref, mbq_ref, mbk_ref, mbv_ref, mbkb_ref, mbvb_ref, gates_ref):
    xn = _rms(x_ref[...], g_ref[...]).astype(BF16)
    c64, s64, c32, s32 = c64_ref[...], s64_ref[...], c32_ref[...], s32_ref[...]

    def sec(name, j=0, width=None):
        lo, hi = _SEC[name]
        if width is not None:
            lo, hi = lo + j * width, lo + (j + 1) * width
        return _dot(xn, w_ref[:, lo:hi])

    for h in range(N_HEADS):
        sl = slice(h * LANES, (h + 1) * LANES)
        sbq_ref[:, sl] = (sec("sbq", h, LANES) * (HEAD_DIM ** -0.5)).astype(BF16)
        qrope_ref[:, sl] = _rope(sec("qrope", h, LANES), c32, s32, MLA_ROPE // 2).astype(BF16)
        mbq_ref[:, sl] = _rope(sec("mbq", h, LANES), c64, s64, HEAD_DIM // 2)

    sbk = sec("sbk")
    sbk_ref[...] = sbk
    sbkb_ref[...] = sbk.astype(BF16)
    sbv = sec("sbv")
    sbv_ref[...] = sbv
    sbvb_ref[...] = sbv.astype(BF16)

    qnope = sec("qnope").astype(BF16)
    qlat_ref[...] = _dot(qnope, wuk_ref[...]).astype(BF16)

    ckv = _rms(sec("ckv"), gkv_ref[...])
    ckv_ref[...] = ckv
    kcat_ref[:, :KV_LORA] = ckv.astype(BF16)
    kr = _rope(sec("kr"), c32, s32, MLA_ROPE // 2)
    kr_ref[...] = kr[:, :MLA_ROPE]
    kcat_ref[:, KV_LORA:] = kr.astype(BF16)

    mbk = _rope(sec("mbk"), c64, s64, HEAD_DIM // 2)
    mbk_ref[...] = mbk
    mbkb_ref[...] = mbk.astype(BF16)
    mbv = sec("mbv")
    mbv_ref[...] = mbv
    mbvb_ref[...] = mbv.astype(BF16)

    for j in range(N_BRANCH * D_MODEL // 512):
        gates_ref[:, j * 512:(j + 1) * 512] = jax.nn.sigmoid(sec("gates", j, 512))


def _proj(x, g, w_all, gkv, wuk_bd, tabs, tm, seq_blocks):
    m = x.shape[0]
    row = lambda w: pl.BlockSpec((tm, w), lambda i: (i, 0))
    tab = pl.BlockSpec((tm, LANES), lambda i: (i % seq_blocks, 0))
    sd = jax.ShapeDtypeStruct
    outs = [(SLOT, BF16), (LANES, F32), (LANES, F32), (LANES, BF16), (LANES, BF16),
            (SLOT, BF16), (SLOT, BF16), (KV_LORA, F32), (MLA_ROPE, F32), (2 * LANES, BF16),
            (SLOT, F32), (LANES, F32), (LANES, F32), (LANES, BF16), (LANES, BF16),
            (N_BRANCH * D_MODEL, F32)]
    return pl.pallas_call(
        _proj_kernel,
        grid=(m // tm,),
        in_specs=[row(D_MODEL), _const_spec((1, D_MODEL)), _const_spec((D_MODEL, W_ALL)),
                  _const_spec((1, KV_LORA)), _const_spec((N_HEADS * MLA_NOPE, SLOT)),
                  tab, tab, tab, tab],
        out_specs=[row(w) for w, _ in outs],
        out_shape=[sd((m, w), dt) for w, dt in outs],
        compiler_params=_cparams(("arbitrary",)),
        name="proj",
    )(x, g, w_all, gkv, wuk_bd, *tabs)


def _stick_terms(z):
    soft = jnp.log(1.0 + jnp.exp(-jnp.abs(z)))
    return jnp.minimum(z, 0.0) - soft, -jnp.maximum(z, 0.0) - soft


def _suffix_sums(lk, u2):
    hi = lk.astype(BF16)
    mid = (lk - hi.astype(F32)).astype(BF16)
    return _dot(jnp.concatenate([hi, mid], axis=1), u2)


def _stack_heads(ref):
    return jnp.concatenate([ref[:, h * LANES:(h + 1) * LANES] for h in range(N_HEADS)], axis=0)


def _unstack_heads(x, o_ref, tq):
    for h in range(N_HEADS):
        o_ref[:, h * LANES:(h + 1) * LANES] = x[h * tq:(h + 1) * tq].astype(o_ref.dtype)


def _sb_prompt_kernel(q_ref, k_ref, v_ref, u_ref, o_ref, *, tq):
    i = pl.program_id(1)
    rows = N_HEADS * tq
    q = _stack_heads(q_ref)
    strict = (lax.broadcasted_iota(jnp.int32, (rows, tq), 1)
              < lax.broadcasted_iota(jnp.int32, (rows, tq), 0) % tq)

    def block(j, carry, acc, diag):
        start = pl.multiple_of(j * tq, tq)
        ls, lk = _stick_terms(_dot_nt(q, k_ref[pl.ds(start, tq), :]))
        if diag:
            lk = jnp.where(strict, lk, 0.0)
        after = _suffix_sums(lk, u_ref[...])
        if not diag:
            after = after + carry
        a = jnp.exp(ls + after)
        if diag:
            a = jnp.where(strict, a, 0.0)
        pv = _dot(a.astype(BF16), v_ref[pl.ds(start, tq), :])
        tot = jnp.sum(lk, axis=-1, keepdims=True)
        return (tot, pv) if diag else (carry + tot, acc + pv)

    def more(st):
        j, _, carry, acc = st
        carry, acc = block(j, carry, acc, False)
        return j - 1, jnp.max(carry), carry, acc

    carry, acc = block(i, None, None, True)
    _, _, _, acc = lax.while_loop(lambda st: (st[0] >= 0) & (st[1] > SB_NEGLIGIBLE), more,
                                  (i - 1, jnp.max(carry), carry, acc))
    _unstack_heads(acc, o_ref, tq)


def _suffix_matrix(tk):
    u = (np.arange(tk)[:, None] > np.arange(tk)[None, :]).astype(np.float32)
    return jnp.asarray(np.concatenate([u, u], axis=0), dtype=BF16)


def _sb_prompt(q, k, v, batch, seq, tq=256):
    nq = seq // tq
    return pl.pallas_call(
        functools.partial(_sb_prompt_kernel, tq=tq),
        grid=(batch, nq),
        in_specs=[pl.BlockSpec((tq, SLOT), lambda b, i: (b * nq + i, 0)),
                  pl.BlockSpec((seq, LANES), lambda b, i: (b, 0)),
                  pl.BlockSpec((seq, LANES), lambda b, i: (b, 0)),
                  _const_spec((2 * tq, tq))],
        out_specs=pl.BlockSpec((tq, SLOT), lambda b, i: (b * nq + i, 0)),
        out_shape=jax.ShapeDtypeStruct((batch * seq, SLOT), BF16),
        compiler_params=_cparams(("arbitrary", "arbitrary")),
        name="sb_prompt",
    )(q, k, v, _suffix_matrix(tq))


def _softmax_block(s, vext, c, state):
    mx = jnp.max(s, axis=-1, keepdims=True)
    if state is None:
        return mx, _dot(jnp.exp2((s - mx) * c).astype(BF16), vext)
    m_old, acc = state
    m_new = jnp.maximum(m_old, mx)
    p = jnp.exp2((s - m_new) * c).astype(BF16)
    return m_new, acc * jnp.exp2((m_old - m_new) * c) + _dot(p, vext)


def _softmax_result(state):
    acc = state[1]
    return acc[:, :LANES] / acc[:, LANES:]


def _mla_prompt_kernel(ql_ref, qr_ref, k_ref, o_ref, *, tq):
    i = pl.program_id(1)
    rows = N_HEADS * tq
    c = (MLA_NOPE + MLA_ROPE) ** -0.5 * LOG2E
    q = jnp.concatenate([_stack_heads(ql_ref), _stack_heads(qr_ref)], axis=1)
    causal = (lax.broadcasted_iota(jnp.int32, (rows, tq), 1)
              <= lax.broadcasted_iota(jnp.int32, (rows, tq), 0) % tq)
    value_lane = lax.broadcasted_iota(jnp.int32, (tq, 2 * LANES), 1) < KV_LORA

    def block(j, state):
        kb = k_ref[pl.ds(pl.multiple_of(j * tq, tq), tq), :]
        vext = jnp.where(value_lane, kb, jnp.ones_like(kb))
        s = _dot_nt(q, kb)
        if state is None:
            s = jnp.where(causal, s, NEG_INF)
        return _softmax_block(s, vext, c, state)

    state = lax.fori_loop(0, i, block, block(i, None))
    _unstack_heads(_softmax_result(state), o_ref, tq)


def _mla_prompt(ql, qr, kcat, batch, seq, tq=256):
    nq = seq // tq
    qspec = pl.BlockSpec((tq, SLOT), lambda b, i: (b * nq + i, 0))
    return pl.pallas_call(
        functools.partial(_mla_prompt_kernel, tq=tq),
        grid=(batch, nq),
        in_specs=[qspec, qspec, pl.BlockSpec((seq, 2 * LANES), lambda b, i: (b, 0))],
        out_specs=qspec,
        out_shape=jax.ShapeDtypeStruct((batch * seq, SLOT), BF16),
        compiler_params=_cparams(("arbitrary", "arbitrary")),
        name="mla_prompt",
    )(ql, qr, kcat)


def _block_mean_kernel(k_ref, o_ref, *, nblk):
    for j in range(nblk):
        blk = k_ref[j * MOBA_BLOCK:(j + 1) * MOBA_BLOCK, :]
        o_ref[j:j + 1, :] = jnp.sum(blk, axis=0, keepdims=True) * (1.0 / MOBA_BLOCK)


def _block_mean(k):
    m = k.shape[0]
    nb = m // MOBA_BLOCK
    nblk = math.gcd(nb, 8)
    return pl.pallas_call(
        functools.partial(_block_mean_kernel, nblk=nblk),
        grid=(nb // nblk,),
        in_specs=[pl.BlockSpec((nblk * MOBA_BLOCK, LANES), lambda i: (i, 0))],
        out_specs=pl.BlockSpec((nblk, LANES), lambda i: (i, 0)),
        out_shape=jax.ShapeDtypeStruct((nb, LANES), F32),
        compiler_params=_cparams(("arbitrary",)),
        name="block_mean",
    )(k)


def _top_blocks(gate, n_valid, axis):
    pos = lax.broadcasted_iota(jnp.int32, gate.shape, axis)
    pos_f = pos.astype(F32)
    valid = pos < n_valid
    cur = jnp.where(valid, gate, NEG_INF)
    sel = jnp.zeros(gate.shape, F32)
    picks = []
    for _ in range(MOBA_TOPK):
        mx = jnp.max(cur, axis=axis, keepdims=True)
        idx = jnp.min(jnp.where(cur == mx, pos_f, 1e9), axis=axis, keepdims=True)
        pick = pos_f == idx
        sel = jnp.maximum(sel, jnp.where(pick & valid, 1.0, 0.0))
        cur = jnp.where(pick, NEG_INF, cur)
        picks.append(idx)
    return sel, picks


def _moba_prompt_kernel(q_ref, k_ref, v_ref, km_ref, o_ref):
    tq = MOBA_BLOCK
    i = pl.program_id(1)
    rows = N_HEADS * tq
    c = HEAD_DIM ** -0.5 * LOG2E
    causal = (lax.broadcasted_iota(jnp.int32, (rows, tq), 1)
              <= lax.broadcasted_iota(jnp.int32, (rows, tq), 0) % tq)
    lane = lax.broadcasted_iota(jnp.int32, (tq, LANES), 1)
    ones = jnp.ones((tq, LANES), BF16)

    qf = _stack_heads(q_ref)
    gate_t = _dot_nt_f32(km_ref[...], qf)
    sel_t, _ = _top_blocks(gate_t, i, 0)
    blk = lax.broadcasted_iota(jnp.int32, gate_t.shape, 0)
    bias = jnp.where((sel_t > 0.0) | (blk == i), 0.0, MASK_BIAS).T
    q = jnp.concatenate([qf.astype(BF16), bias.astype(BF16)], axis=1)

    def block(j, state):
        start = pl.multiple_of(j * tq, tq)
        tag = jnp.where(lane == j, 1.0, 0.0).astype(BF16)
        s = _dot_nt(q, jnp.concatenate([k_ref[pl.ds(start, tq), :], tag], axis=1))
        if state is None:
            s = jnp.where(causal, s, NEG_INF)
        vext = jnp.concatenate([v_ref[pl.ds(start, tq), :], ones], axis=1)
        return _softmax_block(s, vext, c, state)

    state = lax.fori_loop(0, i, block, block(i, None))
    _unstack_heads(_softmax_result(state), o_ref, tq)


def _moba_prompt(q, k, v, kmean, batch, seq):
    nb = seq // MOBA_BLOCK
    assert nb <= LANES
    tq = MOBA_BLOCK
    kmean = jnp.pad(kmean.reshape(batch, nb, LANES), ((0, 0), (0, LANES - nb), (0, 0)))
    qspec = pl.BlockSpec((tq, SLOT), lambda b, i: (b * nb + i, 0))
    kv = pl.BlockSpec((seq, LANES), lambda b, i: (b, 0))
    return pl.pallas_call(
        _moba_prompt_kernel,
        grid=(batch, nb),
        in_specs=[qspec, kv, kv, pl.BlockSpec((LANES, LANES), lambda b, i: (b, 0))],
        out_specs=qspec,
        out_shape=jax.ShapeDtypeStruct((batch * seq, SLOT), BF16),
        compiler_params=_cparams(("arbitrary", "arbitrary")),
        name="moba_prompt",
    )(q, k, v, kmean.reshape(batch * LANES, LANES))


class _Pager:
    def __init__(self, pt_ref, layer, caches, bufs, sems, pages_per_chunk, chunk_of_step):
        self.pt, self.layer, self.caches, self.bufs, self.sems = pt_ref, layer, caches, bufs, sems
        self.p, self.chunk_of_step = pages_per_chunk, chunk_of_step

    def _copies(self, b, chunk, slot):
        out = []
        for cache, buf, sem in zip(self.caches, self.bufs, self.sems):
            for p in range(self.p):
                page = self.pt[b, chunk * self.p + p]
                out.append(pltpu.make_async_copy(cache.at[self.layer, page], buf.at[slot, p],
                                                 sem.at[slot]))
        return out

    def step(self):
        b, c = pl.program_id(0), pl.program_id(1)
        nb, nc = pl.num_programs(0), pl.num_programs(1)
        s = b * nc + c
        slot = s % 2

        @pl.when(s == 0)
        def _():
            for cp in self._copies(b, self.chunk_of_step(c), slot):
                cp.start()

        @pl.when(s + 1 < nb * nc)
        def _():
            wrap = c + 1 == nc
            b2 = jnp.where(wrap, b + 1, b)
            c2 = jnp.where(wrap, 0, c + 1)
            for cp in self._copies(b2, self.chunk_of_step(c2), 1 - slot):
                cp.start()

        for cp in self._copies(b, self.chunk_of_step(c), slot):
            cp.wait()
        return slot


def _sb_sample_kernel(pt_ref, q_ref, u_ref, k_hbm, v_hbm, o_ref, kbuf, vbuf, ksem, vsem, *, layer,
                      ppc):
    b, nb = pl.program_id(0), pl.num_programs(0)
    nchunks = pt_ref.shape[1] // ppc
    home = b % 2

    def copies(bb, chunk, slot):
        base = (nchunks - 1 - chunk) * ppc
        return [pltpu.make_async_copy(cache.at[layer, pt_ref[bb, base + p]], buf.at[slot, p],
                                      sem.at[slot])
                for cache, buf, sem in ((k_hbm, kbuf, ksem), (v_hbm, vbuf, vsem))
                for p in range(ppc)]

    @pl.when(b == 0)
    def _():
        for cp in copies(b, 0, home):
            cp.start()

    @pl.when(b + 1 < nb)
    def _():
        for cp in copies(b + 1, 0, 1 - home):
            cp.start()

    q = q_ref[0]

    def chunk(carry, acc):
        z = jnp.concatenate([_dot(q, kbuf[home, p].astype(BF16)) for p in range(ppc)], axis=0)
        ls, lk = _stick_terms(z)
        within = _suffix_sums(lk, u_ref[...])
        tot = jnp.sum(lk, axis=-1, keepdims=True)
        for p in reversed(range(ppc)):
            r = slice(p * N_HEADS, (p + 1) * N_HEADS)
            a = jnp.exp(ls[r] + within[r] + carry)
            acc = acc + _dot_nt(a.astype(BF16), vbuf[home, p].astype(BF16))
            carry = carry + tot[r]
        return carry, acc

    for cp in copies(b, 0, home):
        cp.wait()
    carry, acc = chunk(jnp.zeros((N_HEADS, 1), F32), jnp.zeros((N_HEADS, LANES), F32))

    def more(st):
        c, _, carry, acc = st
        for cp in copies(b, c, home):
            cp.start()
        for cp in copies(b, c, home):
            cp.wait()
        carry, acc = chunk(carry, acc)
        return c + 1, jnp.max(carry), carry, acc

    _, _, _, acc = lax.while_loop(lambda st: (st[0] < nchunks) & (st[1] > SB_NEGLIGIBLE), more,
                                  (1, jnp.max(carry), carry, acc))
    o_ref[0] = acc.astype(BF16)


def _paged_call(kernel, name, pt, operands, operand_specs, caches, out_dtype, scratch, ppc):
    nseq, npages = pt.shape
    bufs = [pltpu.VMEM((2, ppc) + c.shape[2:], c.dtype) for c in caches]
    sems = [pltpu.SemaphoreType.DMA((2,)) for _ in caches]
    return pl.pallas_call(
        kernel,
        grid_spec=pltpu.PrefetchScalarGridSpec(
            num_scalar_prefetch=1,
            grid=(nseq, npages // ppc),
            in_specs=operand_specs + [pl.BlockSpec(memory_space=pl.ANY) for _ in caches],
            out_specs=pl.BlockSpec((1, N_HEADS, LANES), lambda b, c, pt: (b, 0, 0)),
            scratch_shapes=bufs + sems + scratch),
        out_shape=jax.ShapeDtypeStruct((nseq, N_HEADS, LANES), out_dtype),
        compiler_params=_cparams(("arbitrary", "arbitrary")),
        name=name,
    )(pt, *operands, *caches)


def _head_spec():
    return pl.BlockSpec((1, N_HEADS, LANES), lambda b, c, pt: (b, 0, 0))


def _sb_sample(pt, q, cache_k, cache_v, layer, ppc=8):
    nseq, npages = pt.shape
    assert npages % ppc == 0
    head = pl.BlockSpec((1, N_HEADS, LANES), lambda b, pt: (b, 0, 0))
    out = pl.pallas_call(
        functools.partial(_sb_sample_kernel, layer=layer, ppc=ppc),
        grid_spec=pltpu.PrefetchScalarGridSpec(
            num_scalar_prefetch=1,
            grid=(nseq,),
            in_specs=[head, pl.BlockSpec((2 * PAGE, PAGE), lambda b, pt: (0, 0)),
                      pl.BlockSpec(memory_space=pl.ANY), pl.BlockSpec(memory_space=pl.ANY)],
            out_specs=head,
            scratch_shapes=[pltpu.VMEM((2, ppc, LANES, PAGE), F32),
                            pltpu.VMEM((2, ppc, LANES, PAGE), F32),
                            pltpu.SemaphoreType.DMA((2,)), pltpu.SemaphoreType.DMA((2,))]),
        out_shape=jax.ShapeDtypeStruct((nseq, N_HEADS, LANES), BF16),
        compiler_params=_cparams(("arbitrary",)),
        name="sb_sample",
    )(pt, q.reshape(nseq, N_HEADS, LANES), _suffix_matrix(PAGE), cache_k, cache_v)
    return out.reshape(nseq, SLOT)


def _mla_sample_kernel(pt_ref, ql_ref, qr_ref, cnew_ref, rnew_ref, c_hbm, r_hbm, o_ref, cbuf, rbuf,
                       csem, rsem, m_ref, l_ref, acc_ref, *, layer, ppc):
    c, nc = pl.program_id(1), pl.num_programs(1)
    pager = _Pager(pt_ref, layer, (c_hbm, r_hbm), (cbuf, rbuf), (csem, rsem), ppc, lambda cc: cc)
    slot = pager.step()
    scale = (MLA_NOPE + MLA_ROPE) ** -0.5

    @pl.when(c == 0)
    def _():
        m_ref[...] = jnp.full_like(m_ref, NEG_INF)
        l_ref[...] = jnp.zeros_like(l_ref)
        acc_ref[...] = jnp.zeros_like(acc_ref)

    ql = ql_ref[0]
    qr = qr_ref[0][:, :MLA_ROPE]
    pages = [cbuf[slot, p].astype(BF16) for p in range(ppc)]
    s = [(_dot_nt(ql, pages[p]) + _dot(qr, rbuf[slot, p].astype(BF16))) * scale
         for p in range(ppc)]
    m_old = m_ref[...]
    m_new = jnp.maximum(m_old, jnp.max(functools.reduce(jnp.maximum, s), axis=-1, keepdims=True))
    alpha = jnp.exp(m_old - m_new)
    psum = jnp.zeros((N_HEADS, LANES), F32)
    acc = acc_ref[...] * alpha
    for p in range(ppc):
        pr = jnp.exp(s[p] - m_new)
        psum = psum + pr
        acc = acc + _dot(pr.astype(BF16), pages[p])
    l = l_ref[...] * alpha + jnp.sum(psum, axis=-1, keepdims=True)
    m_ref[...] = m_new
    l_ref[...] = l
    acc_ref[...] = acc

    @pl.when(c == nc - 1)
    def _():
        cnew = cnew_ref[0]
        s_new = (jnp.sum(ql.astype(F32) * cnew.astype(BF16).astype(F32), axis=-1, keepdims=True)
                 + jnp.sum(qr.astype(F32) * rnew_ref[0].astype(BF16).astype(F32), axis=-1,
                           keepdims=True)) * scale
        m_fin = jnp.maximum(m_new, s_new)
        a2 = jnp.exp(m_new - m_fin)
        p_new = jnp.exp(s_new - m_fin)
        l_fin = l * a2 + p_new
        o = (acc * a2 + p_new.astype(BF16).astype(F32) * cnew.astype(BF16).astype(F32)) / l_fin
        o_ref[0] = o.astype(BF16)


def _mla_sample(pt, ql, qr, ckv_new, kr_new, cache_c, cache_r, layer, ppc=32):
    nseq = pt.shape[0]
    scratch = [pltpu.VMEM((N_HEADS, 1), F32), pltpu.VMEM((N_HEADS, 1), F32),
               pltpu.VMEM((N_HEADS, LANES), F32)]
    out = _paged_call(
        functools.partial(_mla_sample_kernel, layer=layer, ppc=ppc), "mla_sample", pt,
        [ql.reshape(nseq, N_HEADS, LANES), qr.reshape(nseq, N_HEADS, LANES),
         ckv_new.reshape(nseq, 1, KV_LORA), kr_new.reshape(nseq, 1, MLA_ROPE)],
        [_head_spec(), _head_spec(),
         pl.BlockSpec((1, 1, KV_LORA), lambda b, c, pt: (b, 0, 0)),
         pl.BlockSpec((1, 1, MLA_ROPE), lambda b, c, pt: (b, 0, 0))],
        [cache_c, cache_r], BF16, scratch, ppc)
    return out.reshape(nseq, SLOT)


def _moba_select_kernel(pt_ref, q_ref, k_hbm, o_ref, kbuf, ksem, km_ref, *, layer, ppc, n_blocks):
    c, nc = pl.program_id(1), pl.num_programs(1)
    pager = _Pager(pt_ref, layer, (k_hbm,), (kbuf,), (ksem,), ppc, lambda cc: cc)
    slot = pager.step()
    ppb = MOBA_BLOCK // PAGE
    bpc = ppc // ppb
    lane_blk = lax.broadcasted_iota(jnp.int32, (LANES, LANES), 1)

    @pl.when(c == 0)
    def _():
        km_ref[...] = jnp.zeros_like(km_ref)

    km = km_ref[...]
    for j in range(bpc):
        tot = kbuf[slot, j * ppb]
        for t in range(1, ppb):
            tot = tot + kbuf[slot, j * ppb + t]
        mean = jnp.sum(tot, axis=1, keepdims=True) * (1.0 / MOBA_BLOCK)
        km = jnp.where(lane_blk == c * bpc + j, mean, km)
    km_ref[...] = km

    @pl.when(c == nc - 1)
    def _():
        gate = _dot_f32(q_ref[0], km)
        _, picks = _top_blocks(gate, n_blocks, 1)
        lane = lax.broadcasted_iota(jnp.int32, (N_HEADS, LANES), 1)
        out = jnp.zeros((N_HEADS, LANES), F32)
        for r, idx in enumerate(picks):
            out = jnp.where(lane == r, idx, out)
        o_ref[0] = out.astype(jnp.int32)


def _moba_select(pt, q, cache_k, layer, ppc=32):
    nseq, npages = pt.shape
    n_blocks = npages * PAGE // MOBA_BLOCK
    assert MOBA_TOPK <= n_blocks <= LANES
    scratch = [pltpu.VMEM((LANES, LANES), F32)]
    return _paged_call(
        functools.partial(_moba_select_kernel, layer=layer, ppc=ppc, n_blocks=n_blocks),
        "moba_select", pt,
        [q.reshape(nseq, N_HEADS, LANES)], [_head_spec()], [cache_k], jnp.int32, scratch, ppc)


def _moba_sample_kernel(pg_ref, q_ref, knew_ref, vnew_ref, k_hbm, v_hbm, o_ref, kbuf, vbuf, ksem,
                        vsem, *, layer, npg):
    b, nb = pl.program_id(0), pl.num_programs(0)
    slot = b % 2
    per_head = npg // N_HEADS
    scale = HEAD_DIM ** -0.5

    def copies(bb, sl):
        out = []
        for cache, buf, sem in ((k_hbm, kbuf, ksem), (v_hbm, vbuf, vsem)):
            for p in range(npg):
                out.append(pltpu.make_async_copy(cache.at[layer, pg_ref[bb, p]], buf.at[sl, p],
                                                 sem.at[sl]))
        return out

    @pl.when(b == 0)
    def _():
        for cp in copies(b, slot):
            cp.start()

    @pl.when(b + 1 < nb)
    def _():
        for cp in copies(b + 1, 1 - slot):
            cp.start()

    for cp in copies(b, slot):
        cp.wait()

    qf = q_ref[0]
    qb = qf.astype(BF16)
    row = lax.broadcasted_iota(jnp.int32, (N_HEADS, LANES), 0)
    s = []
    for p in range(npg):
        sp = _dot(qb, kbuf[slot, p].astype(BF16)) * scale
        s.append(jnp.where(row == p // per_head, sp, NEG_INF))
    knew = knew_ref[0].astype(BF16).astype(F32)
    s_new = jnp.sum(qb.astype(F32) * knew, axis=-1, keepdims=True) * scale
    m = jnp.maximum(jnp.max(functools.reduce(jnp.maximum, s), axis=-1, keepdims=True), s_new)
    p_new = jnp.exp(s_new - m)
    psum = jnp.zeros((N_HEADS, LANES), F32)
    acc = p_new.astype(BF16).astype(F32) * vnew_ref[0].astype(BF16).astype(F32)
    for p in range(npg):
        pr = jnp.exp(s[p] - m)
        psum = psum + pr
        acc = acc + _dot_nt(pr.astype(BF16), vbuf[slot, p].astype(BF16))
    l = jnp.sum(psum, axis=-1, keepdims=True) + p_new
    o_ref[0] = (acc / l).astype(BF16)


def _moba_sample(pages, q, k_new, v_new, cache_k, cache_v, layer):
    nseq, npg = pages.shape
    head = pl.BlockSpec((1, N_HEADS, LANES), lambda b, pg: (b, 0, 0))
    new = pl.BlockSpec((1, 1, LANES), lambda b, pg: (b, 0, 0))
    out = pl.pallas_call(
        functools.partial(_moba_sample_kernel, layer=layer, npg=npg),
        grid_spec=pltpu.PrefetchScalarGridSpec(
            num_scalar_prefetch=1,
            grid=(nseq,),
            in_specs=[head, new, new, pl.BlockSpec(memory_space=pl.ANY),
                      pl.BlockSpec(memory_space=pl.ANY)],
            out_specs=head,
            scratch_shapes=[pltpu.VMEM((2, npg, PAGE, LANES), F32),
                            pltpu.VMEM((2, npg, PAGE, LANES), F32),
                            pltpu.SemaphoreType.DMA((2,)), pltpu.SemaphoreType.DMA((2,))]),
        out_shape=jax.ShapeDtypeStruct((nseq, N_HEADS, LANES), BF16),
        compiler_params=_cparams(("arbitrary",)),
        name="moba_sample",
    )(pages, q.reshape(nseq, N_HEADS, LANES), k_new.reshape(nseq, 1, LANES),
      v_new.reshape(nseq, 1, LANES), cache_k, cache_v)
    return out.reshape(nseq, SLOT)


def _merge_kernel(x_ref, osb_ref, olat_ref, omb_ref, g_ref, wsb_ref, wuv_ref, wmla_ref, wmb_ref,
                  wout_ref, o_ref):
    d = D_MODEL
    a = _dot(osb_ref[...], wsb_ref[...])
    b = _dot(_dot(olat_ref[...], wuv_ref[...]).astype(BF16), wmla_ref[...])
    c = _dot(omb_ref[...], wmb_ref[...])
    merged = g_ref[:, :d] * a + g_ref[:, d:2 * d] * b + g_ref[:, 2 * d:] * c
    o_ref[...] = x_ref[...] + _dot(merged.astype(BF16), wout_ref[...])


def _merge(x, osb, olat, omb, gates, wsb, wuv, wmla, wmb, wout, tm):
    m = x.shape[0]
    row = lambda w: pl.BlockSpec((tm, w), lambda i: (i, 0))
    return pl.pallas_call(
        _merge_kernel,
        grid=(m // tm,),
        in_specs=[row(D_MODEL), row(SLOT), row(SLOT), row(SLOT), row(N_BRANCH * D_MODEL),
                  _const_spec(wsb.shape), _const_spec(wuv.shape), _const_spec(wmla.shape),
                  _const_spec(wmb.shape), _const_spec(wout.shape)],
        out_specs=row(D_MODEL),
        out_shape=jax.ShapeDtypeStruct((m, D_MODEL), F32),
        compiler_params=_cparams(("arbitrary",)),
        name="merge",
    )(x, osb, olat, omb, gates, wsb, wuv, wmla, wmb, wout)


def _pad_heads(w, group_of_head=True):
    lead = w.shape[:-1]
    w = w.reshape(*lead, N_HEADS, 1, HEAD_DIM)
    half = (jnp.arange(N_HEADS)[:, None] // GROUP == jnp.arange(KV_HEADS)[None, :]).astype(w.dtype)
    return (w * half[:, :, None]).reshape(*lead, SLOT)


def _layer_weights(l, w_in, w_uk, w_uv, w_o_sb, w_o_mla, w_o_moba, w_out):
    w = w_in[l]
    d = D_MODEL
    o = np.cumsum([0, 512, 128, 128, 768, 128, 32, 512, 128, 128, 3072])
    sbq, sbk, sbv, mq, ckv, kr, mbq, mbk, mbv, gates = [w[:, o[i]:o[i + 1]] for i in range(10)]
    mq = mq.reshape(d, N_HEADS, MLA_NOPE + MLA_ROPE)
    qnope = mq[:, :, :MLA_NOPE].reshape(d, N_HEADS * MLA_NOPE)
    qrope = jnp.pad(mq[:, :, MLA_NOPE:], ((0, 0), (0, 0), (0, LANES - MLA_ROPE))).reshape(d, SLOT)
    kr = jnp.pad(kr, ((0, 0), (0, LANES - MLA_ROPE)))
    w_all = jnp.concatenate([_pad_heads(sbq), sbk, sbv, qnope, qrope, ckv, kr, _pad_heads(mbq),
                             mbk, mbv, gates], axis=1).astype(BF16)
    eye = jnp.eye(N_HEADS, dtype=F32)
    wuk_bd = (w_uk[l].transpose(1, 2, 0)[:, :, None, :] * eye[:, None, :, None]).reshape(
        N_HEADS * MLA_NOPE, SLOT).astype(BF16)
    wuv_bd = (w_uv[l].transpose(1, 0, 2)[:, :, None, :] * eye[:, None, :, None]).reshape(
        SLOT, N_HEADS * HEAD_DIM).astype(BF16)
    wsb = _pad_heads(w_o_sb[l].T).T.astype(BF16)
    wmb = _pad_heads(w_o_moba[l].T).T.astype(BF16)
    return dict(w_all=w_all, wuk=wuk_bd, wuv=wuv_bd, wsb=wsb, wmla=w_o_mla[l].astype(BF16), wmb=wmb,
                wout=w_out[l].astype(BF16))


def _rope_tables(pos):
    def one(dim):
        half = dim // 2
        inv = jnp.exp(-math.log(ROPE_THETA) * 2.0 * jnp.arange(half, dtype=F32) / dim)
        ang = pos.astype(F32)[:, None] * inv[None, :]
        cos, sin = jnp.cos(ang), jnp.sin(ang)
        reps = LANES // dim
        return (jnp.tile(jnp.concatenate([cos, cos], axis=1), (1, reps)),
                jnp.tile(jnp.concatenate([-sin, sin], axis=1), (1, reps)))
    c64, s64 = one(HEAD_DIM)
    c32, s32 = one(MLA_ROPE)
    return c64, s64, c32, s32


def kernel(x_prompt, x_sample, cache_sb_k, cache_sb_v, cache_mla_ckv, cache_mla_krope, cache_moba_k, cache_moba_v, page_table, g_ffn1, w_ffn1_gu, w_ffn1_down, g_mix, w_in, g_kv, w_uk, w_uv, w_o_sb, w_o_mla, w_o_moba, w_out, g_ffn2, w_ffn2_gu, w_ffn2_down, g_final):
    batch, seq, d = x_prompt.shape
    nseq, dec_seq, _ = x_sample.shape
    assert dec_seq == 1 and d == D_MODEL
    depth = w_in.shape[0]
    n_pool = cache_sb_k.shape[1]
    npages = page_table.shape[1]
    past_len = npages * PAGE
    assert past_len % MOBA_BLOCK == 0 and seq % MOBA_BLOCK == 0
    mp, ms = batch * seq, nseq
    tm_p, tm_s = 512, ms

    tabs_p = _rope_tables(jnp.arange(seq, dtype=jnp.int32))
    tabs_s = _rope_tables(jnp.full((ms,), past_len, dtype=jnp.int32))
    feat_major = lambda c: jnp.transpose(c, (0, 1, 3, 4, 2)).reshape(depth, n_pool, LANES, PAGE)
    csb_k, csb_v = feat_major(cache_sb_k), feat_major(cache_sb_v)
    cmb_k, cmb_v = feat_major(cache_moba_k), feat_major(cache_moba_v)
    ckr = jnp.transpose(cache_mla_krope, (0, 1, 3, 2))
    gf = g_final.reshape(1, d)

    hp = x_prompt.reshape(mp, d)
    hs = x_sample.reshape(ms, d)
    rows_p, rows_s = [], []
    for l in range(depth):
        lw = _layer_weights(l, w_in, w_uk, w_uv, w_o_sb, w_o_mla, w_o_moba, w_out)
        f1 = (g_ffn1[l].reshape(1, d), w_ffn1_gu[l][:, :D_FF].astype(BF16),
              w_ffn1_gu[l][:, D_FF:].astype(BF16), w_ffn1_down[l].astype(BF16), gf)
        f2 = (g_ffn2[l].reshape(1, d), w_ffn2_gu[l][:, :D_FF].astype(BF16),
              w_ffn2_gu[l][:, D_FF:].astype(BF16), w_ffn2_down[l].astype(BF16), gf)
        last = l == depth - 1
        gm, gkv = g_mix[l].reshape(1, d), g_kv[l].reshape(1, KV_LORA)
        merge_w = (lw["wsb"], lw["wuv"], lw["wmla"], lw["wmb"], lw["wout"])

        hp = _ffn(hp, *f1, tm_p, False)
        (sbq, sbk, sbv, sbkb, sbvb, qlat, qrope, ckv, kr, kcat, mbq, mbk, mbv, mbkb, mbvb,
         gates) = _proj(hp, gm, lw["w_all"], gkv, lw["wuk"], tabs_p, tm_p, seq // tm_p)
        osb = _sb_prompt(sbq, sbkb, sbvb, batch, seq)
        olat = _mla_prompt(qlat, qrope, kcat, batch, seq)
        kmean = _block_mean(mbk)
        omb = _moba_prompt(mbq, mbkb, mbvb, kmean, batch, seq)
        hp = _merge(hp, osb, olat, omb, gates, *merge_w, tm_p)
        hp = _ffn(hp, *f2, tm_p, last)
        rows_p.append((sbk.reshape(batch, seq, KV_HEADS, HEAD_DIM),
                       sbv.reshape(batch, seq, KV_HEADS, HEAD_DIM),
                       ckv.reshape(batch, seq, KV_LORA), kr.reshape(batch, seq, MLA_ROPE),
                       mbk.reshape(batch, seq, KV_HEADS, HEAD_DIM),
                       mbv.reshape(batch, seq, KV_HEADS, HEAD_DIM)))

        hs = _ffn(hs, *f1, tm_s, False)
        (sbq, sbk, sbv, _, _, qlat, qrope, ckv, kr, _, mbq, mbk, mbv, _, _,
         gates) = _proj(hs, gm, lw["w_all"], gkv, lw["wuk"], tabs_s, tm_s, 1)
        osb = _sb_sample(page_table, sbq, csb_k, csb_v, l)
        olat = _mla_sample(page_table, qlat, qrope, ckv, kr, cache_mla_ckv, ckr, l)
        sel = _moba_select(page_table, mbq, cmb_k, l)[:, :, :MOBA_TOPK]
        ppb = MOBA_BLOCK // PAGE
        page_idx = (sel[..., None] * ppb + jnp.arange(ppb, dtype=jnp.int32)).reshape(nseq, -1)
        pages = jnp.take_along_axis(page_table, page_idx, axis=1)
        omb = _moba_sample(pages, mbq, mbk, mbv, cmb_k, cmb_v, l)
        hs = _merge(hs, osb, olat, omb, gates, *merge_w, tm_s)
        hs = _ffn(hs, *f2, tm_s, last)
        rows_s.append((sbk.reshape(nseq, 1, KV_HEADS, HEAD_DIM),
                       sbv.reshape(nseq, 1, KV_HEADS, HEAD_DIM),
                       ckv.reshape(nseq, 1, KV_LORA), kr.reshape(nseq, 1, MLA_ROPE),
                       mbk.reshape(nseq, 1, KV_HEADS, HEAD_DIM),
                       mbv.reshape(nseq, 1, KV_HEADS, HEAD_DIM)))

    y_prompt = hp.reshape(batch, seq, d)
    y_sample = hs.reshape(nseq, 1, d)
    stacked_p = [jnp.stack(r) for r in zip(*rows_p)]
    stacked_s = [jnp.stack(r) for r in zip(*rows_s)]
    return (y_prompt, y_sample, *stacked_p, *stacked_s)
```

```python
import functools
import math

import jax
import jax.numpy as jnp
import numpy as np
from jax import lax
from jax.experimental import pallas as pl
from jax.experimental.pallas import tpu as pltpu

F32 = jnp.float32
BF16 = jnp.bfloat16

D_MODEL = 1024
HEAD_DIM = 64
N_HEADS = 8
KV_HEADS = 2
GROUP = N_HEADS // KV_HEADS
MLA_NOPE = 64
MLA_ROPE = 32
KV_LORA = 128
MOBA_BLOCK = 256
MOBA_TOPK = 3
N_BRANCH = 3
D_FF = 2816
ROPE_THETA = 10000.0
EPS = 1e-6
PAGE = 128
LANES = 128
SLOT = N_HEADS * LANES

V7X_VMEM_BYTES = 64 * 1024 * 1024
VMEM_LIMIT = 52 * 1024 * 1024

NEG_INF = float("-inf")
LOG2E = math.log2(math.e)
SB_NEGLIGIBLE = -110.0
MASK_BIAS = -1e30


def _cparams(sem):
    return pltpu.CompilerParams(dimension_semantics=sem, vmem_limit_bytes=VMEM_LIMIT)


def _dot(a, b):
    return jnp.dot(a, b, preferred_element_type=F32)


def _dot_nt(a, b):
    return lax.dot_general(a, b, (((1,), (1,)), ((), ())), preferred_element_type=F32)


def _split3(x):
    hi = x.astype(BF16)
    r = x - hi.astype(F32)
    mid = r.astype(BF16)
    lo = (r - mid.astype(F32)).astype(BF16)
    return hi, mid, lo


def _dot_f32(a, b, dot=_dot):
    a1, a2, a3 = _split3(a)
    b1, b2, b3 = _split3(b)
    return (dot(a1, b1) + (dot(a1, b2) + dot(a2, b1))
            + (dot(a1, b3) + dot(a2, b2) + dot(a3, b1)))


def _dot_nt_f32(a, b):
    return _dot_f32(a, b, _dot_nt)


def _rms(x, g):
    return x * lax.rsqrt(jnp.mean(x * x, axis=-1, keepdims=True) + EPS) * g


def _rope(x, cos, sin_signed, half):
    lane = lax.broadcasted_iota(jnp.int32, x.shape, 1)
    first = (lane % (2 * half)) < half
    rot = jnp.where(first, pltpu.roll(x, LANES - half, 1), pltpu.roll(x, half, 1))
    return x * cos + rot * sin_signed


def _const_spec(shape):
    return pl.BlockSpec(shape, lambda *_: (0,) * len(shape))


def _ffn_kernel(x_ref, g_ref, wg_ref, wu_ref, wd_ref, gf_ref, o_ref, h_ref, *, tf, final_norm):
    x = x_ref[...]
    xn = _rms(x, g_ref[...]).astype(BF16)
    for c in range(D_FF // tf):
        sl = slice(c * tf, (c + 1) * tf)
        a = _dot(xn, wg_ref[:, sl])
        b = _dot(xn, wu_ref[:, sl])
        h_ref[:, sl] = (a * jax.nn.sigmoid(a) * b).astype(BF16)
    y = x + 0.5 * _dot(h_ref[...], wd_ref[...])
    if final_norm:
        y = _rms(y, gf_ref[...])
    o_ref[...] = y


def _ffn(x, g, wg, wu, wd, g_final, tm, final_norm):
    m = x.shape[0]
    kern = functools.partial(_ffn_kernel, tf=256, final_norm=final_norm)
    return pl.pallas_call(
        kern,
        grid=(m // tm,),
        in_specs=[pl.BlockSpec((tm, D_MODEL), lambda i: (i, 0)),
                  _const_spec((1, D_MODEL)),
                  _const_spec((D_MODEL, D_FF)),
                  _const_spec((D_MODEL, D_FF)),
                  _const_spec((D_FF, D_MODEL)),
                  _const_spec((1, D_MODEL))],
        out_specs=pl.BlockSpec((tm, D_MODEL), lambda i: (i, 0)),
        out_shape=jax.ShapeDtypeStruct((m, D_MODEL), F32),
        scratch_shapes=[pltpu.VMEM((tm, D_FF), BF16)],
        compiler_params=_cparams(("arbitrary",)),
        name="ffn",
    )(x, g, wg, wu, wd, g_final)


_SEC = {}
_off = 0
for _name, _w in (("sbq", SLOT), ("sbk", LANES), ("sbv", LANES), ("qnope", N_HEADS * MLA_NOPE),
                  ("qrope", SLOT), ("ckv", KV_LORA), ("kr", LANES), ("mbq", SLOT), ("mbk", LANES),
                  ("mbv", LANES), ("gates", N_BRANCH * D_MODEL)):
    _SEC[_name] = (_off, _off + _w)
    _off += _w
W_ALL = _off


def _proj_kernel(x_ref, g_ref, w_ref, gkv_ref, wuk_ref, c64_ref, s64_ref, c32_ref, s32_ref,
                 sbq_ref, sbk_ref, sbv_ref, sbkb_ref, sbvb_ref, qlat_ref, qrope_ref, ckv_ref,
                 kr_ref, kcat_ref, mbq_ref, mbk_ref, mbv_ref, mbkb_ref, mbvb_ref, gates_ref):
    xn = _rms(x_ref[...], g_ref[...]).astype(BF16)
    c64, s64, c32, s32 = c64_ref[...], s64_ref[...], c32_ref[...], s32_ref[...]

    def sec(name, j=0, width=None):
        lo, hi = _SEC[name]
        if width is not None:
            lo, hi = lo + j * width, lo + (j + 1) * width
        return _dot(xn, w_ref[:, lo:hi])

    for h in range(N_HEADS):
        sl = slice(h * LANES, (h + 1) * LANES)
        sbq_ref[:, sl] = (sec("sbq", h, LANES) * (HEAD_DIM ** -0.5)).astype(BF16)
        qrope_ref[:, sl] = _rope(sec("qrope", h, LANES), c32, s32, MLA_ROPE // 2).astype(BF16)
        mbq_ref[:, sl] = _rope(sec("mbq", h, LANES), c64, s64, HEAD_DIM // 2)

    sbk = sec("sbk")
    sbk_ref[...] = sbk
    sbkb_ref[...] = sbk.astype(BF16)
    sbv = sec("sbv")
    sbv_ref[...] = sbv
    sbvb_ref[...] = sbv.astype(BF16)

    qnope = sec("qnope").astype(BF16)
    qlat_ref[...] = _dot(qnope, wuk_ref[...]).astype(BF16)

    ckv = _rms(sec("ckv"), gkv_ref[...])
    ckv_ref[...] = ckv
    kcat_ref[:, :KV_LORA] = ckv.astype(BF16)
    kr = _rope(sec("kr"), c32, s32, MLA_ROPE // 2)
    kr_ref[...] = kr[:, :MLA_ROPE]
    kcat_ref[:, KV_LORA:] = kr.astype(BF16)

    mbk = _rope(sec("mbk"), c64, s64, HEAD_DIM // 2)
    mbk_ref[...] = mbk
    mbkb_ref[...] = mbk.astype(BF16)
    mbv = sec("mbv")
    mbv_ref[...] = mbv
    mbvb_ref[...] = mbv.astype(BF16)

    for j in range(N_BRANCH * D_MODEL // 512):
        gates_ref[:, j * 512:(j + 1) * 512] = jax.nn.sigmoid(sec("gates", j, 512))


def _proj(x, g, w_all, gkv, wuk_bd, tabs, tm, seq_blocks):
    m = x.shape[0]
    row = lambda w: pl.BlockSpec((tm, w), lambda i: (i, 0))
    tab = pl.BlockSpec((tm, LANES), lambda i: (i % seq_blocks, 0))
    sd = jax.ShapeDtypeStruct
    outs = [(SLOT, BF16), (LANES, F32), (LANES, F32), (LANES, BF16), (LANES, BF16),
            (SLOT, BF16), (SLOT, BF16), (KV_LORA, F32), (MLA_ROPE, F32), (2 * LANES, BF16),
            (SLOT, F32), (LANES, F32), (LANES, F32), (LANES, BF16), (LANES, BF16),
            (N_BRANCH * D_MODEL, F32)]
    return pl.pallas_call(
        _proj_kernel,
        grid=(m // tm,),
        in_specs=[row(D_MODEL), _const_spec((1, D_MODEL)), _const_spec((D_MODEL, W_ALL)),
                  _const_spec((1, KV_LORA)), _const_spec((N_HEADS * MLA_NOPE, SLOT)),
                  tab, tab, tab, tab],
        out_specs=[row(w) for w, _ in outs],
        out_shape=[sd((m, w), dt) for w, dt in outs],
        compiler_params=_cparams(("arbitrary",)),
        name="proj",
    )(x, g, w_all, gkv, wuk_bd, *tabs)


def _stick_terms(z):
    soft = jnp.log(1.0 + jnp.exp(-jnp.abs(z)))
    return jnp.minimum(z, 0.0) - soft, -jnp.maximum(z, 0.0) - soft


def _suffix_sums(lk, u2):
    hi = lk.astype(BF16)
    mid = (lk - hi.astype(F32)).astype(BF16)
    return _dot(jnp.concatenate([hi, mid], axis=1), u2)


def _stack_heads(ref):
    return jnp.concatenate([ref[:, h * LANES:(h + 1) * LANES] for h in range(N_HEADS)], axis=0)


def _unstack_heads(x, o_ref, tq):
    for h in range(N_HEADS):
        o_ref[:, h * LANES:(h + 1) * LANES] = x[h * tq:(h + 1) * tq].astype(o_ref.dtype)


def _sb_prompt_kernel(q_ref, k_ref, v_ref, u_ref, o_ref, *, tq):
    i = pl.program_id(1)
    rows = N_HEADS * tq
    q = _stack_heads(q_ref)
    strict = (lax.broadcasted_iota(jnp.int32, (rows, tq), 1)
              < lax.broadcasted_iota(jnp.int32, (rows, tq), 0) % tq)

    def block(j, carry, acc, diag):
        start = pl.multiple_of(j * tq, tq)
        ls, lk = _stick_terms(_dot_nt(q, k_ref[pl.ds(start, tq), :]))
        if diag:
            lk = jnp.where(strict, lk, 0.0)
        after = _suffix_sums(lk, u_ref[...])
        if not diag:
            after = after + carry
        a = jnp.exp(ls + after)
        if diag:
            a = jnp.where(strict, a, 0.0)
        pv = _dot(a.astype(BF16), v_ref[pl.ds(start, tq), :])
        tot = jnp.sum(lk, axis=-1, keepdims=True)
        return (tot, pv) if diag else (carry + tot, acc + pv)

    def more(st):
        j, _, carry, acc = st
        carry, acc = block(j, carry, acc, False)
        return j - 1, jnp.max(carry), carry, acc

    carry, acc = block(i, None, None, True)
    _, _, _, acc = lax.while_loop(lambda st: (st[0] >= 0) & (st[1] > SB_NEGLIGIBLE), more,
                                  (i - 1, jnp.max(carry), carry, acc))
    _unstack_heads(acc, o_ref, tq)


def _suffix_matrix(tk):
    u = (np.arange(tk)[:, None] > np.arange(tk)[None, :]).astype(np.float32)
    return jnp.asarray(np.concatenate([u, u], axis=0), dtype=BF16)


def _sb_prompt(q, k, v, batch, seq, tq=256):
    nq = seq // tq
    return pl.pallas_call(
        functools.partial(_sb_prompt_kernel, tq=tq),
        grid=(batch, nq),
        in_specs=[pl.BlockSpec((tq, SLOT), lambda b, i: (b * nq + i, 0)),
                  pl.BlockSpec((seq, LANES), lambda b, i: (b, 0)),
                  pl.BlockSpec((seq, LANES), lambda b, i: (b, 0)),
                  _const_spec((2 * tq, tq))],
        out_specs=pl.BlockSpec((tq, SLOT), lambda b, i: (b * nq + i, 0)),
        out_shape=jax.ShapeDtypeStruct((batch * seq, SLOT), BF16),
        compiler_params=_cparams(("arbitrary", "arbitrary")),
        name="sb_prompt",
    )(q, k, v, _suffix_matrix(tq))


def _softmax_block(s, vext, c, state):
    mx = jnp.max(s, axis=-1, keepdims=True)
    if state is None:
        return mx, _dot(jnp.exp2((s - mx) * c).astype(BF16), vext)
    m_old, acc = state
    m_new = jnp.maximum(m_old, mx)
    p = jnp.exp2((s - m_new) * c).astype(BF16)
    return m_new, acc * jnp.exp2((m_old - m_new) * c) + _dot(p, vext)


def _softmax_result(state):
    acc = state[1]
    return acc[:, :LANES] / acc[:, LANES:]


def _mla_prompt_kernel(ql_ref, qr_ref, k_ref, o_ref, *, tq):
    i = pl.program_id(1)
    rows = N_HEADS * tq
    c = (MLA_NOPE + MLA_ROPE) ** -0.5 * LOG2E
    q = jnp.concatenate([_stack_heads(ql_ref), _stack_heads(qr_ref)], axis=1)
    causal = (lax.broadcasted_iota(jnp.int32, (rows, tq), 1)
              <= lax.broadcasted_iota(jnp.int32, (rows, tq), 0) % tq)
    value_lane = lax.broadcasted_iota(jnp.int32, (tq, 2 * LANES), 1) < KV_LORA

    def block(j, state):
        kb = k_ref[pl.ds(pl.multiple_of(j * tq, tq), tq), :]
        vext = jnp.where(value_lane, kb, jnp.ones_like(kb))
        s = _dot_nt(q, kb)
        if state is None:
            s = jnp.where(causal, s, NEG_INF)
        return _softmax_block(s, vext, c, state)

    state = lax.fori_loop(0, i, block, block(i, None))
    _unstack_heads(_softmax_result(state), o_ref, tq)


def _mla_prompt(ql, qr, kcat, batch, seq, tq=256):
    nq = seq // tq
    qspec = pl.BlockSpec((tq, SLOT), lambda b, i: (b * nq + i, 0))
    return pl.pallas_call(
        functools.partial(_mla_prompt_kernel, tq=tq),
        grid=(batch, nq),
        in_specs=[qspec, qspec, pl.BlockSpec((seq, 2 * LANES), lambda b, i: (b, 0))],
        out_specs=qspec,
        out_shape=jax.ShapeDtypeStruct((batch * seq, SLOT), BF16),
        compiler_params=_cparams(("arbitrary", "arbitrary")),
        name="mla_prompt",
    )(ql, qr, kcat)


def _block_mean_kernel(k_ref, o_ref, *, nblk):
    for j in range(nblk):
        blk = k_ref[j * MOBA_BLOCK:(j + 1) * MOBA_BLOCK, :]
        o_ref[j:j + 1, :] = jnp.sum(blk, axis=0, keepdims=True) * (1.0 / MOBA_BLOCK)


def _block_mean(k):
    m = k.shape[0]
    nb = m // MOBA_BLOCK
    nblk = math.gcd(nb, 8)
    return pl.pallas_call(
        functools.partial(_block_mean_kernel, nblk=nblk),
        grid=(nb // nblk,),
        in_specs=[pl.BlockSpec((nblk * MOBA_BLOCK, LANES), lambda i: (i, 0))],
        out_specs=pl.BlockSpec((nblk, LANES), lambda i: (i, 0)),
        out_shape=jax.ShapeDtypeStruct((nb, LANES), F32),
        compiler_params=_cparams(("arbitrary",)),
        name="block_mean",
    )(k)


def _top_blocks(gate, n_valid, axis):
    pos = lax.broadcasted_iota(jnp.int32, gate.shape, axis)
    pos_f = pos.astype(F32)
    valid = pos < n_valid
    cur = jnp.where(valid, gate, NEG_INF)
    sel = jnp.zeros(gate.shape, F32)
    picks = []
    for _ in range(MOBA_TOPK):
        mx = jnp.max(cur, axis=axis, keepdims=True)
        idx = jnp.min(jnp.where(cur == mx, pos_f, 1e9), axis=axis, keepdims=True)
        pick = pos_f == idx
        sel = jnp.maximum(sel, jnp.where(pick & valid, 1.0, 0.0))
        cur = jnp.where(pick, NEG_INF, cur)
        picks.append(idx)
    return sel, picks


def _moba_prompt_kernel(q_ref, k_ref, v_ref, km_ref, o_ref, *, nbp):
    tq = MOBA_BLOCK
    i = pl.program_id(1)
    rows = N_HEADS * tq
    c = HEAD_DIM ** -0.5 * LOG2E
    causal = (lax.broadcasted_iota(jnp.int32, (rows, tq), 1)
              <= lax.broadcasted_iota(jnp.int32, (rows, tq), 0) % tq)
    lane = lax.broadcasted_iota(jnp.int32, (tq, LANES), 1)
    ones = jnp.ones((tq, LANES), BF16)

    qf = _stack_heads(q_ref)
    gate_t = _dot_nt_f32(km_ref[...], qf)[:nbp]
    sel_t, _ = _top_blocks(gate_t, i, 0)
    blk = lax.broadcasted_iota(jnp.int32, gate_t.shape, 0)
    bias_t = jnp.where((sel_t > 0.0) | (blk == i), 0.0, MASK_BIAS)
    bias = jnp.concatenate([bias_t, jnp.zeros((LANES - nbp, rows), F32)], axis=0).T
    q = jnp.concatenate([qf.astype(BF16), bias.astype(BF16)], axis=1)

    def block(j, state):
        start = pl.multiple_of(j * tq, tq)
        tag = jnp.where(lane == j, 1.0, 0.0).astype(BF16)
        s = _dot_nt(q, jnp.concatenate([k_ref[pl.ds(start, tq), :], tag], axis=1))
        if state is None:
            s = jnp.where(causal, s, NEG_INF)
        vext = jnp.concatenate([v_ref[pl.ds(start, tq), :], ones], axis=1)
        return _softmax_block(s, vext, c, state)

    state = lax.fori_loop(0, i, block, block(i, None))
    _unstack_heads(_softmax_result(state), o_ref, tq)


def _moba_prompt(q, k, v, kmean, batch, seq):
    nb = seq // MOBA_BLOCK
    assert nb <= LANES
    tq = MOBA_BLOCK
    kmean = jnp.pad(kmean.reshape(batch, nb, LANES), ((0, 0), (0, LANES - nb), (0, 0)))
    qspec = pl.BlockSpec((tq, SLOT), lambda b, i: (b * nb + i, 0))
    kv = pl.BlockSpec((seq, LANES), lambda b, i: (b, 0))
    return pl.pallas_call(
        functools.partial(_moba_prompt_kernel, nbp=-(-nb // 8) * 8),
        grid=(batch, nb),
        in_specs=[qspec, kv, kv, pl.BlockSpec((LANES, LANES), lambda b, i: (b, 0))],
        out_specs=qspec,
        out_shape=jax.ShapeDtypeStruct((batch * seq, SLOT), BF16),
        compiler_params=_cparams(("arbitrary", "arbitrary")),
        name="moba_prompt",
    )(q, k, v, kmean.reshape(batch * LANES, LANES))


class _Pager:
    def __init__(self, pt_ref, layer, caches, bufs, sems, pages_per_chunk):
        self.pt, self.layer, self.caches, self.bufs, self.sems = pt_ref, layer, caches, bufs, sems
        self.p = pages_per_chunk
        self.b, self.c = pl.program_id(0), pl.program_id(1)
        self.nb, self.nc = pl.num_programs(0), pl.num_programs(1)
        self.s = self.b * self.nc + self.c
        self.slot = self.s % 2

    def _copies(self, b, chunk, slot):
        out = []
        for cache, buf, sem in zip(self.caches, self.bufs, self.sems):
            for p in range(self.p):
                page = self.pt[b, chunk * self.p + p]
                out.append(pltpu.make_async_copy(cache.at[self.layer, page], buf.at[slot, p],
                                                 sem.at[slot]))
        return out

    def step(self):
        @pl.when(self.s == 0)
        def _():
            for cp in self._copies(self.b, self.c, self.slot):
                cp.start()

        for cp in self._copies(self.b, self.c, self.slot):
            cp.wait()
        last_chunk = self.c + 1 == self.nc
        b2 = jnp.where(last_chunk, jnp.where(self.b + 1 == self.nb, 0, self.b + 1), self.b)
        c2 = jnp.where(last_chunk, 0, self.c + 1)
        for cp in self._copies(b2, c2, 1 - self.slot):
            cp.start()
        return self.slot

    def finish(self):
        @pl.when(self.s + 1 == self.nb * self.nc)
        def _():
            for cp in self._copies(0, 0, 1 - self.slot):
                cp.wait()


def _sb_sample_kernel(pt_ref, q_ref, u_ref, k_hbm, v_hbm, o_ref, kbuf, vbuf, ksem, vsem, *, layer,
                      ppc):
    b, nb = pl.program_id(0), pl.num_programs(0)
    nchunks = pt_ref.shape[1] // ppc
    home = b % 2

    def copies(bb, chunk, slot):
        base = (nchunks - 1 - chunk) * ppc
        return [pltpu.make_async_copy(cache.at[layer, pt_ref[bb, base + p]], buf.at[slot, p],
                                      sem.at[slot])
                for cache, buf, sem in ((k_hbm, kbuf, ksem), (v_hbm, vbuf, vsem))
                for p in range(ppc)]

    @pl.when(b == 0)
    def _():
        for cp in copies(b, 0, home):
            cp.start()

    @pl.when(b + 1 < nb)
    def _():
        for cp in copies(b + 1, 0, 1 - home):
            cp.start()

    q = q_ref[0]

    def chunk(carry, acc):
        z = jnp.concatenate([_dot(q, kbuf[home, p].astype(BF16)) for p in range(ppc)], axis=0)
        ls, lk = _stick_terms(z)
        within = _suffix_sums(lk, u_ref[...])
        tot = jnp.sum(lk, axis=-1, keepdims=True)
        for p in reversed(range(ppc)):
            r = slice(p * N_HEADS, (p + 1) * N_HEADS)
            a = jnp.exp(ls[r] + within[r] + carry)
            acc = acc + _dot_nt(a.astype(BF16), vbuf[home, p].astype(BF16))
            carry = carry + tot[r]
        return carry, acc

    for cp in copies(b, 0, home):
        cp.wait()
    carry, acc = chunk(jnp.zeros((N_HEADS, 1), F32), jnp.zeros((N_HEADS, LANES), F32))

    def more(st):
        c, _, carry, acc = st
        for cp in copies(b, c, home):
            cp.start()
        for cp in copies(b, c, home):
            cp.wait()
        carry, acc = chunk(carry, acc)
        return c + 1, jnp.max(carry), carry, acc

    _, _, _, acc = lax.while_loop(lambda st: (st[0] < nchunks) & (st[1] > SB_NEGLIGIBLE), more,
                                  (1, jnp.max(carry), carry, acc))
    o_ref[0] = acc.astype(BF16)


def _sb_sample(pt, q, cache_k, cache_v, layer, ppc=8):
    nseq, npages = pt.shape
    assert npages % ppc == 0
    head = pl.BlockSpec((1, N_HEADS, LANES), lambda b, pt: (b, 0, 0))
    out = pl.pallas_call(
        functools.partial(_sb_sample_kernel, layer=layer, ppc=ppc),
        grid_spec=pltpu.PrefetchScalarGridSpec(
            num_scalar_prefetch=1,
            grid=(nseq,),
            in_specs=[head, pl.BlockSpec((2 * PAGE, PAGE), lambda b, pt: (0, 0)),
                      pl.BlockSpec(memory_space=pl.ANY), pl.BlockSpec(memory_space=pl.ANY)],
            out_specs=head,
            scratch_shapes=[pltpu.VMEM((2, ppc, LANES, PAGE), F32),
                            pltpu.VMEM((2, ppc, LANES, PAGE), F32),
                            pltpu.SemaphoreType.DMA((2,)), pltpu.SemaphoreType.DMA((2,))]),
        out_shape=jax.ShapeDtypeStruct((nseq, N_HEADS, LANES), BF16),
        compiler_params=_cparams(("arbitrary",)),
        name="sb_sample",
    )(pt, q.reshape(nseq, N_HEADS, LANES), _suffix_matrix(PAGE), cache_k, cache_v)
    return out.reshape(nseq, SLOT)


def _stream_sample_kernel(pt_ref, ql_ref, qr_ref, cnew_ref, rnew_ref, mq_ref, c_hbm, r_hbm, mk_hbm,
                          o_ref, sel_ref, cbuf, rbuf, mkbuf, csem, rsem, mksem, m_ref, l_ref,
                          acc_ref, km_ref, *, layer, ppc, n_blocks):
    c, nc = pl.program_id(1), pl.num_programs(1)
    pager = _Pager(pt_ref, layer, (c_hbm, r_hbm, mk_hbm), (cbuf, rbuf, mkbuf),
                   (csem, rsem, mksem), ppc)
    slot = pager.step()

    @pl.when(c == 0)
    def _():
        km_ref[...] = jnp.zeros_like(km_ref)
        m_ref[...] = jnp.full_like(m_ref, NEG_INF)
        l_ref[...] = jnp.zeros_like(l_ref)
        acc_ref[...] = jnp.zeros_like(acc_ref)

    km = _moba_means_chunk(c, slot, mkbuf, km_ref, ppc)
    ql, qr, m, l, acc = _mla_sample_chunk(slot, ql_ref, qr_ref, cbuf, rbuf, m_ref, l_ref, acc_ref,
                                          ppc)

    @pl.when(c == nc - 1)
    def _():
        _moba_pick_blocks(mq_ref, km, sel_ref, n_blocks)
        _mla_sample_finish(ql, qr, m, l, acc, cnew_ref, rnew_ref, o_ref)

    pager.finish()


def _mla_sample_chunk(slot, ql_ref, qr_ref, cbuf, rbuf, m_ref, l_ref, acc_ref, ppc):
    scale = (MLA_NOPE + MLA_ROPE) ** -0.5
    ql = ql_ref[0]
    qr = qr_ref[0][:, :MLA_ROPE]
    pages = [cbuf[slot, p].astype(BF16) for p in range(ppc)]
    s = [(_dot_nt(ql, pages[p]) + _dot(qr, rbuf[slot, p].astype(BF16))) * scale
         for p in range(ppc)]
    m_old = m_ref[...]
    m_new = jnp.maximum(m_old, jnp.max(functools.reduce(jnp.maximum, s), axis=-1, keepdims=True))
    alpha = jnp.exp(m_old - m_new)
    psum = jnp.zeros((N_HEADS, LANES), F32)
    acc = acc_ref[...] * alpha
    for p in range(ppc):
        pr = jnp.exp(s[p] - m_new)
        psum = psum + pr
        acc = acc + _dot(pr.astype(BF16), pages[p])
    l = l_ref[...] * alpha + jnp.sum(psum, axis=-1, keepdims=True)
    m_ref[...] = m_new
    l_ref[...] = l
    acc_ref[...] = acc
    return ql, qr, m_new, l, acc


def _mla_sample_finish(ql, qr, m, l, acc, cnew_ref, rnew_ref, o_ref):
    scale = (MLA_NOPE + MLA_ROPE) ** -0.5
    cnew = cnew_ref[0].astype(BF16).astype(F32)
    rnew = rnew_ref[0].astype(BF16).astype(F32)
    s_new = (jnp.sum(ql.astype(F32) * cnew, axis=-1, keepdims=True)
             + jnp.sum(qr.astype(F32) * rnew, axis=-1, keepdims=True)) * scale
    m_fin = jnp.maximum(m, s_new)
    a2 = jnp.exp(m - m_fin)
    p_new = jnp.exp(s_new - m_fin)
    o = (acc * a2 + p_new.astype(BF16).astype(F32) * cnew) / (l * a2 + p_new)
    o_ref[0] = o.astype(BF16)


def _stream_sample(pt, ql, qr, ckv_new, kr_new, mq, cache_c, cache_r, cache_mk, layer, ppc=32):
    nseq, npages = pt.shape
    n_blocks = npages * PAGE // MOBA_BLOCK
    assert npages % ppc == 0 and MOBA_TOPK <= n_blocks <= LANES
    caches = [cache_c, cache_r, cache_mk]
    head = pl.BlockSpec((1, N_HEADS, LANES), lambda b, c, pt: (b, 0, 0))
    new = lambda w: pl.BlockSpec((1, 1, w), lambda b, c, pt: (b, 0, 0))
    out, sel = pl.pallas_call(
        functools.partial(_stream_sample_kernel, layer=layer, ppc=ppc, n_blocks=n_blocks),
        grid_spec=pltpu.PrefetchScalarGridSpec(
            num_scalar_prefetch=1,
            grid=(nseq, npages // ppc),
            in_specs=[head, head, new(KV_LORA), new(MLA_ROPE), head]
            + [pl.BlockSpec(memory_space=pl.ANY) for _ in caches],
            out_specs=[head, head],
            scratch_shapes=[pltpu.VMEM((2, ppc) + c.shape[2:], c.dtype) for c in caches]
            + [pltpu.SemaphoreType.DMA((2,)) for _ in caches]
            + [pltpu.VMEM((N_HEADS, 1), F32), pltpu.VMEM((N_HEADS, 1), F32),
               pltpu.VMEM((N_HEADS, LANES), F32), pltpu.VMEM((LANES, LANES), F32)]),
        out_shape=[jax.ShapeDtypeStruct((nseq, N_HEADS, LANES), BF16),
                   jax.ShapeDtypeStruct((nseq, N_HEADS, LANES), jnp.int32)],
        compiler_params=_cparams(("arbitrary", "arbitrary")),
        name="stream_sample",
    )(pt, ql.reshape(nseq, N_HEADS, LANES), qr.reshape(nseq, N_HEADS, LANES),
      ckv_new.reshape(nseq, 1, KV_LORA), kr_new.reshape(nseq, 1, MLA_ROPE),
      mq.reshape(nseq, N_HEADS, LANES), *caches)
    return out.reshape(nseq, SLOT), sel


def _moba_means_chunk(c, slot, kbuf, km_ref, ppc):
    ppb = MOBA_BLOCK // PAGE
    bpc = ppc // ppb
    lane_blk = lax.broadcasted_iota(jnp.int32, (LANES, LANES), 1)
    km = km_ref[...]
    for j in range(bpc):
        tot = kbuf[slot, j * ppb]
        for t in range(1, ppb):
            tot = tot + kbuf[slot, j * ppb + t]
        mean = jnp.sum(tot, axis=1, keepdims=True) * (1.0 / MOBA_BLOCK)
        km = jnp.where(lane_blk == c * bpc + j, mean, km)
    km_ref[...] = km
    return km


def _moba_pick_blocks(q_ref, km, o_ref, n_blocks):
    gate = _dot_f32(q_ref[0], km)
    _, picks = _top_blocks(gate, n_blocks, 1)
    lane = lax.broadcasted_iota(jnp.int32, (N_HEADS, LANES), 1)
    out = jnp.zeros((N_HEADS, LANES), F32)
    for r, idx in enumerate(picks):
        out = jnp.where(lane == r, idx, out)
    o_ref[0] = out.astype(jnp.int32)


def _moba_sample_kernel(pg_ref, q_ref, knew_ref, vnew_ref, k_hbm, v_hbm, o_ref, kbuf, vbuf, ksem,
                        vsem, *, layer, npg):
    b, nb = pl.program_id(0), pl.num_programs(0)
    slot = b % 2
    per_head = npg // N_HEADS
    scale = HEAD_DIM ** -0.5

    def copies(bb, sl):
        out = []
        for cache, buf, sem in ((k_hbm, kbuf, ksem), (v_hbm, vbuf, vsem)):
            for p in range(npg):
                out.append(pltpu.make_async_copy(cache.at[layer, pg_ref[bb, p]], buf.at[sl, p],
                                                 sem.at[sl]))
        return out

    @pl.when(b == 0)
    def _():
        for cp in copies(b, slot):
            cp.start()

    for cp in copies(b, slot):
        cp.wait()
    nxt = jnp.where(b + 1 == nb, 0, b + 1)
    for cp in copies(nxt, 1 - slot):
        cp.start()

    qf = q_ref[0]
    qb = qf.astype(BF16)
    row = lax.broadcasted_iota(jnp.int32, (N_HEADS, LANES), 0)
    s = []
    for p in range(npg):
        sp = _dot(qb, kbuf[slot, p].astype(BF16)) * scale
        s.append(jnp.where(row == p // per_head, sp, NEG_INF))
    knew = knew_ref[0].astype(BF16).astype(F32)
    s_new = jnp.sum(qb.astype(F32) * knew, axis=-1, keepdims=True) * scale
    m = jnp.maximum(jnp.max(functools.reduce(jnp.maximum, s), axis=-1, keepdims=True), s_new)
    p_new = jnp.exp(s_new - m)
    psum = jnp.zeros((N_HEADS, LANES), F32)
    acc = p_new.astype(BF16).astype(F32) * vnew_ref[0].astype(BF16).astype(F32)
    for p in range(npg):
        pr = jnp.exp(s[p] - m)
        psum = psum + pr
        acc = acc + _dot_nt(pr.astype(BF16), vbuf[slot, p].astype(BF16))
    l = jnp.sum(psum, axis=-1, keepdims=True) + p_new
    o_ref[0] = (acc / l).astype(BF16)

    @pl.when(b + 1 == nb)
    def _():
        for cp in copies(0, 1 - slot):
            cp.wait()


def _moba_sample(pages, q, k_new, v_new, cache_k, cache_v, layer):
    nseq, npg = pages.shape
    head = pl.BlockSpec((1, N_HEADS, LANES), lambda b, pg: (b, 0, 0))
    new = pl.BlockSpec((1, 1, LANES), lambda b, pg: (b, 0, 0))
    out = pl.pallas_call(
        functools.partial(_moba_sample_kernel, layer=layer, npg=npg),
        grid_spec=pltpu.PrefetchScalarGridSpec(
            num_scalar_prefetch=1,
            grid=(nseq,),
            in_specs=[head, new, new, pl.BlockSpec(memory_space=pl.ANY),
                      pl.BlockSpec(memory_space=pl.ANY)],
            out_specs=head,
            scratch_shapes=[pltpu.VMEM((2, npg, PAGE, LANES), F32),
                            pltpu.VMEM((2, npg, PAGE, LANES), F32),
                            pltpu.SemaphoreType.DMA((2,)), pltpu.SemaphoreType.DMA((2,))]),
        out_shape=jax.ShapeDtypeStruct((nseq, N_HEADS, LANES), BF16),
        compiler_params=_cparams(("arbitrary",)),
        name="moba_sample",
    )(pages, q.reshape(nseq, N_HEADS, LANES), k_new.reshape(nseq, 1, LANES),
      v_new.reshape(nseq, 1, LANES), cache_k, cache_v)
    return out.reshape(nseq, SLOT)


def _merge_kernel(x_ref, osb_ref, olat_ref, omb_ref, g_ref, wsb_ref, wuv_ref, wmla_ref, wmb_ref,
                  wout_ref, o_ref):
    d = D_MODEL
    a = _dot(osb_ref[...], wsb_ref[...])
    b = _dot(_dot(olat_ref[...], wuv_ref[...]).astype(BF16), wmla_ref[...])
    c = _dot(omb_ref[...], wmb_ref[...])
    merged = g_ref[:, :d] * a + g_ref[:, d:2 * d] * b + g_ref[:, 2 * d:] * c
    o_ref[...] = x_ref[...] + _dot(merged.astype(BF16), wout_ref[...])


def _merge(x, osb, olat, omb, gates, wsb, wuv, wmla, wmb, wout, tm):
    m = x.shape[0]
    row = lambda w: pl.BlockSpec((tm, w), lambda i: (i, 0))
    return pl.pallas_call(
        _merge_kernel,
        grid=(m // tm,),
        in_specs=[row(D_MODEL), row(SLOT), row(SLOT), row(SLOT), row(N_BRANCH * D_MODEL),
                  _const_spec(wsb.shape), _const_spec(wuv.shape), _const_spec(wmla.shape),
                  _const_spec(wmb.shape), _const_spec(wout.shape)],
        out_specs=row(D_MODEL),
        out_shape=jax.ShapeDtypeStruct((m, D_MODEL), F32),
        compiler_params=_cparams(("arbitrary",)),
        name="merge",
    )(x, osb, olat, omb, gates, wsb, wuv, wmla, wmb, wout)


def _pad_heads(w, group_of_head=True):
    lead = w.shape[:-1]
    w = w.reshape(*lead, N_HEADS, 1, HEAD_DIM)
    half = (jnp.arange(N_HEADS)[:, None] // GROUP == jnp.arange(KV_HEADS)[None, :]).astype(w.dtype)
    return (w * half[:, :, None]).reshape(*lead, SLOT)


def _layer_weights(l, w_in, w_uk, w_uv, w_o_sb, w_o_mla, w_o_moba, w_out):
    w = w_in[l]
    d = D_MODEL
    o = np.cumsum([0, 512, 128, 128, 768, 128, 32, 512, 128, 128, 3072])
    sbq, sbk, sbv, mq, ckv, kr, mbq, mbk, mbv, gates = [w[:, o[i]:o[i + 1]] for i in range(10)]
    mq = mq.reshape(d, N_HEADS, MLA_NOPE + MLA_ROPE)
    qnope = mq[:, :, :MLA_NOPE].reshape(d, N_HEADS * MLA_NOPE)
    qrope = jnp.pad(mq[:, :, MLA_NOPE:], ((0, 0), (0, 0), (0, LANES - MLA_ROPE))).reshape(d, SLOT)
    kr = jnp.pad(kr, ((0, 0), (0, LANES - MLA_ROPE)))
    w_all = jnp.concatenate([_pad_heads(sbq), sbk, sbv, qnope, qrope, ckv, kr, _pad_heads(mbq),
                             mbk, mbv, gates], axis=1).astype(BF16)
    eye = jnp.eye(N_HEADS, dtype=F32)
    wuk_bd = (w_uk[l].transpose(1, 2, 0)[:, :, None, :] * eye[:, None, :, None]).reshape(
        N_HEADS * MLA_NOPE, SLOT).astype(BF16)
    wuv_bd = (w_uv[l].transpose(1, 0, 2)[:, :, None, :] * eye[:, None, :, None]).reshape(
        SLOT, N_HEADS * HEAD_DIM).astype(BF16)
    wsb = _pad_heads(w_o_sb[l].T).T.astype(BF16)
    wmb = _pad_heads(w_o_moba[l].T).T.astype(BF16)
    return dict(w_all=w_all, wuk=wuk_bd, wuv=wuv_bd, wsb=wsb, wmla=w_o_mla[l].astype(BF16), wmb=wmb,
                wout=w_out[l].astype(BF16))


def _rope_tables(pos):
    def one(dim):
        half = dim // 2
        inv = jnp.exp(-math.log(ROPE_THETA) * 2.0 * jnp.arange(half, dtype=F32) / dim)
        ang = pos.astype(F32)[:, None] * inv[None, :]
        cos, sin = jnp.cos(ang), jnp.sin(ang)
        reps = LANES // dim
        return (jnp.tile(jnp.concatenate([cos, cos], axis=1), (1, reps)),
                jnp.tile(jnp.concatenate([-sin, sin], axis=1), (1, reps)))
    c64, s64 = one(HEAD_DIM)
    c32, s32 = one(MLA_ROPE)
    return c64, s64, c32, s32


def kernel(x_prompt, x_sample, cache_sb_k, cache_sb_v, cache_mla_ckv, cache_mla_krope, cache_moba_k, cache_moba_v, page_table, g_ffn1, w_ffn1_gu, w_ffn1_down, g_mix, w_in, g_kv, w_uk, w_uv, w_o_sb, w_o_mla, w_o_moba, w_out, g_ffn2, w_ffn2_gu, w_ffn2_down, g_final):
    batch, seq, d = x_prompt.shape
    nseq, dec_seq, _ = x_sample.shape
    assert dec_seq == 1 and d == D_MODEL
    depth = w_in.shape[0]
    n_pool = cache_sb_k.shape[1]
    npages = page_table.shape[1]
    past_len = npages * PAGE
    assert past_len % MOBA_BLOCK == 0 and seq % MOBA_BLOCK == 0
    mp, ms = batch * seq, nseq
    tm_p, tm_s = 512, ms

    tabs_p = _rope_tables(jnp.arange(seq, dtype=jnp.int32))
    tabs_s = _rope_tables(jnp.full((ms,), past_len, dtype=jnp.int32))
    feat_major = lambda c: jnp.transpose(c, (0, 1, 3, 4, 2)).reshape(depth, n_pool, LANES, PAGE)
    csb_k, csb_v = feat_major(cache_sb_k), feat_major(cache_sb_v)
    cmb_k, cmb_v = feat_major(cache_moba_k), feat_major(cache_moba_v)
    ckr = jnp.transpose(cache_mla_krope, (0, 1, 3, 2))
    gf = g_final.reshape(1, d)

    hp = x_prompt.reshape(mp, d)
    hs = x_sample.reshape(ms, d)
    rows_p, rows_s = [], []
    for l in range(depth):
        lw = _layer_weights(l, w_in, w_uk, w_uv, w_o_sb, w_o_mla, w_o_moba, w_out)
        f1 = (g_ffn1[l].reshape(1, d), w_ffn1_gu[l][:, :D_FF].astype(BF16),
              w_ffn1_gu[l][:, D_FF:].astype(BF16), w_ffn1_down[l].astype(BF16), gf)
        f2 = (g_ffn2[l].reshape(1, d), w_ffn2_gu[l][:, :D_FF].astype(BF16),
              w_ffn2_gu[l][:, D_FF:].astype(BF16), w_ffn2_down[l].astype(BF16), gf)
        last = l == depth - 1
        gm, gkv = g_mix[l].reshape(1, d), g_kv[l].reshape(1, KV_LORA)
        merge_w = (lw["wsb"], lw["wuv"], lw["wmla"], lw["wmb"], lw["wout"])

        hp = _ffn(hp, *f1, tm_p, False)
        (sbq, sbk, sbv, sbkb, sbvb, qlat, qrope, ckv, kr, kcat, mbq, mbk, mbv, mbkb, mbvb,
         gates) = _proj(hp, gm, lw["w_all"], gkv, lw["wuk"], tabs_p, tm_p, seq // tm_p)
        osb = _sb_prompt(sbq, sbkb, sbvb, batch, seq)
        olat = _mla_prompt(qlat, qrope, kcat, batch, seq)
        kmean = _block_mean(mbk)
        omb = _moba_prompt(mbq, mbkb, mbvb, kmean, batch, seq)
        hp = _merge(hp, osb, olat, omb, gates, *merge_w, tm_p)
        hp = _ffn(hp, *f2, tm_p, last)
        rows_p.append((sbk.reshape(batch, seq, KV_HEADS, HEAD_DIM),
                       sbv.reshape(batch, seq, KV_HEADS, HEAD_DIM),
                       ckv.reshape(batch, seq, KV_LORA), kr.reshape(batch, seq, MLA_ROPE),
                       mbk.reshape(batch, seq, KV_HEADS, HEAD_DIM),
                       mbv.reshape(batch, seq, KV_HEADS, HEAD_DIM)))

        hs = _ffn(hs, *f1, tm_s, False)
        (sbq, sbk, sbv, _, _, qlat, qrope, ckv, kr, _, mbq, mbk, mbv, _, _,
         gates) = _proj(hs, gm, lw["w_all"], gkv, lw["wuk"], tabs_s, tm_s, 1)
        osb = _sb_sample(page_table, sbq, csb_k, csb_v, l)
        olat, sel = _stream_sample(page_table, qlat, qrope, ckv, kr, mbq, cache_mla_ckv, ckr, cmb_k, l)
        sel = sel[:, :, :MOBA_TOPK]
        ppb = MOBA_BLOCK // PAGE
        page_idx = (sel[..., None] * ppb + jnp.arange(ppb, dtype=jnp.int32)).reshape(nseq, -1)
        pages = jnp.take_along_axis(page_table, page_idx, axis=1)
        omb = _moba_sample(pages, mbq, mbk, mbv, cmb_k, cmb_v, l)
        hs = _merge(hs, osb, olat, omb, gates, *merge_w, tm_s)
        hs = _ffn(hs, *f2, tm_s, last)
        rows_s.append((sbk.reshape(nseq, 1, KV_HEADS, HEAD_DIM),
                       sbv.reshape(nseq, 1, KV_HEADS, HEAD_DIM),
                       ckv.reshape(nseq, 1, KV_LORA), kr.reshape(nseq, 1, MLA_ROPE),
                       mbk.reshape(nseq, 1, KV_HEADS, HEAD_DIM),
                       mbv.reshape(nseq, 1, KV_HEADS, HEAD_DIM)))

    y_prompt = hp.reshape(batch, seq, d)
    y_sample = hs.reshape(nseq, 1, d)
    stacked_p = [jnp.stack(r) for r in zip(*rows_p)]
    stacked_s = [jnp.stack(r) for r in zip(*rows_s)]
    return (y_prompt, y_sample, *stacked_p, *stacked_s)
```

```python
import functools
import math

import jax
import jax.numpy as jnp
import numpy as np
from jax import lax
from jax.experimental import pallas as pl
from jax.experimental.pallas import tpu as pltpu

F32 = jnp.float32
BF16 = jnp.bfloat16

D_MODEL = 1024
HEAD_DIM = 64
N_HEADS = 8
KV_HEADS = 2
GROUP = N_HEADS // KV_HEADS
MLA_NOPE = 64
MLA_ROPE = 32
KV_LORA = 128
MOBA_BLOCK = 256
MOBA_TOPK = 3
N_BRANCH = 3
D_FF = 2816
ROPE_THETA = 10000.0
EPS = 1e-6
PAGE = 128
LANES = 128
SLOT = N_HEADS * LANES

V7X_VMEM_BYTES = 64 * 1024 * 1024
VMEM_LIMIT = 52 * 1024 * 1024

NEG_INF = float("-inf")
LOG2E = math.log2(math.e)
SB_NEGLIGIBLE = -110.0
MASK_BIAS = -1e30


def _cparams(sem):
    return pltpu.CompilerParams(dimension_semantics=sem, vmem_limit_bytes=VMEM_LIMIT)


def _dot(a, b):
    return jnp.dot(a, b, preferred_element_type=F32)


def _dot_nt(a, b):
    return lax.dot_general(a, b, (((1,), (1,)), ((), ())), preferred_element_type=F32)


def _split3(x):
    hi = x.astype(BF16)
    r = x - hi.astype(F32)
    mid = r.astype(BF16)
    lo = (r - mid.astype(F32)).astype(BF16)
    return hi, mid, lo


def _dot_f32(a, b, dot=_dot):
    a1, a2, a3 = _split3(a)
    b1, b2, b3 = _split3(b)
    return (dot(a1, b1) + (dot(a1, b2) + dot(a2, b1))
            + (dot(a1, b3) + dot(a2, b2) + dot(a3, b1)))


def _dot_nt_f32(a, b):
    return _dot_f32(a, b, _dot_nt)


def _rms(x, g):
    return x * lax.rsqrt(jnp.mean(x * x, axis=-1, keepdims=True) + EPS) * g


def _rope(x, cos, sin_signed, half):
    lane = lax.broadcasted_iota(jnp.int32, x.shape, 1)
    first = (lane % (2 * half)) < half
    rot = jnp.where(first, pltpu.roll(x, LANES - half, 1), pltpu.roll(x, half, 1))
    return x * cos + rot * sin_signed


def _const_spec(shape):
    return pl.BlockSpec(shape, lambda *_: (0,) * len(shape))


def _ffn_kernel(x_ref, g_ref, wg_ref, wu_ref, wd_ref, gf_ref, o_ref, h_ref, *, tf, final_norm):
    x = x_ref[...]
    xn = _rms(x, g_ref[...]).astype(BF16)
    for c in range(D_FF // tf):
        sl = slice(c * tf, (c + 1) * tf)
        a = _dot(xn, wg_ref[:, sl])
        b = _dot(xn, wu_ref[:, sl])
        h_ref[:, sl] = (a * jax.nn.sigmoid(a) * b).astype(BF16)
    y = x + 0.5 * _dot(h_ref[...], wd_ref[...])
    if final_norm:
        y = _rms(y, gf_ref[...])
    o_ref[...] = y


def _ffn(x, g, wg, wu, wd, g_final, tm, final_norm):
    m = x.shape[0]
    kern = functools.partial(_ffn_kernel, tf=256, final_norm=final_norm)
    return pl.pallas_call(
        kern,
        grid=(m // tm,),
        in_specs=[pl.BlockSpec((tm, D_MODEL), lambda i: (i, 0)),
                  _const_spec((1, D_MODEL)),
                  _const_spec((D_MODEL, D_FF)),
                  _const_spec((D_MODEL, D_FF)),
                  _const_spec((D_FF, D_MODEL)),
                  _const_spec((1, D_MODEL))],
        out_specs=pl.BlockSpec((tm, D_MODEL), lambda i: (i, 0)),
        out_shape=jax.ShapeDtypeStruct((m, D_MODEL), F32),
        scratch_shapes=[pltpu.VMEM((tm, D_FF), BF16)],
        compiler_params=_cparams(("arbitrary",)),
        name="ffn",
    )(x, g, wg, wu, wd, g_final)


_SEC = {}
_off = 0
for _name, _w in (("sbq", SLOT), ("sbk", LANES), ("sbv", LANES), ("qnope", N_HEADS * MLA_NOPE),
                  ("qrope", SLOT), ("ckv", KV_LORA), ("kr", LANES), ("mbq", SLOT), ("mbk", LANES),
                  ("mbv", LANES), ("gates", N_BRANCH * D_MODEL)):
    _SEC[_name] = (_off, _off + _w)
    _off += _w
W_ALL = _off


def _proj_kernel(x_ref, g_ref, w_ref, gkv_ref, wuk_ref, c64_ref, s64_ref, c32_ref, s32_ref,
                 sbq_ref, sbk_ref, sbv_ref, sbkb_ref, sbvb_ref, qlat_ref, qrope_ref, ckv_ref,
                 kr_ref, kcat_ref, mbq_ref, mbk_ref, mbv_ref, mbkb_ref, mbvb_ref, gates_ref):
    xn = _rms(x_ref[...], g_ref[...]).astype(BF16)
    c64, s64, c32, s32 = c64_ref[...], s64_ref[...], c32_ref[...], s32_ref[...]

    def sec(name, j=0, width=None):
        lo, hi = _SEC[name]
        if width is not None:
            lo, hi = lo + j * width, lo + (j + 1) * width
        return _dot(xn, w_ref[:, lo:hi])

    for h in range(N_HEADS):
        sl = slice(h * LANES, (h + 1) * LANES)
        sbq_ref[:, sl] = (sec("sbq", h, LANES) * (HEAD_DIM ** -0.5)).astype(BF16)
        qrope_ref[:, sl] = _rope(sec("qrope", h, LANES), c32, s32, MLA_ROPE // 2).astype(BF16)
        mbq_ref[:, sl] = _rope(sec("mbq", h, LANES), c64, s64, HEAD_DIM // 2)

    sbk = sec("sbk")
    sbk_ref[...] = sbk
    sbkb_ref[...] = sbk.astype(BF16)
    sbv = sec("sbv")
    sbv_ref[...] = sbv
    sbvb_ref[...] = sbv.astype(BF16)

    qnope = sec("qnope").astype(BF16)
    qlat_ref[...] = _dot(qnope, wuk_ref[...]).astype(BF16)

    ckv = _rms(sec("ckv"), gkv_ref[...])
    ckv_ref[...] = ckv
    kcat_ref[:, :KV_LORA] = ckv.astype(BF16)
    kr = _rope(sec("kr"), c32, s32, MLA_ROPE // 2)
    kr_ref[...] = kr[:, :MLA_ROPE]
    kcat_ref[:, KV_LORA:] = kr.astype(BF16)

    mbk = _rope(sec("mbk"), c64, s64, HEAD_DIM // 2)
    mbk_ref[...] = mbk
    mbkb_ref[...] = mbk.astype(BF16)
    mbv = sec("mbv")
    mbv_ref[...] = mbv
    mbvb_ref[...] = mbv.astype(BF16)

    for j in range(N_BRANCH * D_MODEL // 512):
        gates_ref[:, j * 512:(j + 1) * 512] = jax.nn.sigmoid(sec("gates", j, 512))


def _proj(x, g, w_all, gkv, wuk_bd, tabs, tm, seq_blocks):
    m = x.shape[0]
    row = lambda w: pl.BlockSpec((tm, w), lambda i: (i, 0))
    tab = pl.BlockSpec((tm, LANES), lambda i: (i % seq_blocks, 0))
    sd = jax.ShapeDtypeStruct
    outs = [(SLOT, BF16), (LANES, F32), (LANES, F32), (LANES, BF16), (LANES, BF16),
            (SLOT, BF16), (SLOT, BF16), (KV_LORA, F32), (MLA_ROPE, F32), (2 * LANES, BF16),
            (SLOT, F32), (LANES, F32), (LANES, F32), (LANES, BF16), (LANES, BF16),
            (N_BRANCH * D_MODEL, F32)]
    return pl.pallas_call(
        _proj_kernel,
        grid=(m // tm,),
        in_specs=[row(D_MODEL), _const_spec((1, D_MODEL)), _const_spec((D_MODEL, W_ALL)),
                  _const_spec((1, KV_LORA)), _const_spec((N_HEADS * MLA_NOPE, SLOT)),
                  tab, tab, tab, tab],
        out_specs=[row(w) for w, _ in outs],
        out_shape=[sd((m, w), dt) for w, dt in outs],
        compiler_params=_cparams(("arbitrary",)),
        name="proj",
    )(x, g, w_all, gkv, wuk_bd, *tabs)


def _stick_terms(z):
    soft = jnp.log(1.0 + jnp.exp(-jnp.abs(z)))
    return jnp.minimum(z, 0.0) - soft, -jnp.maximum(z, 0.0) - soft


def _suffix_sums(lk, u2):
    hi = lk.astype(BF16)
    mid = (lk - hi.astype(F32)).astype(BF16)
    return _dot(jnp.concatenate([hi, mid], axis=1), u2)


def _stack_heads(ref):
    return jnp.concatenate([ref[:, h * LANES:(h + 1) * LANES] for h in range(N_HEADS)], axis=0)


def _unstack_heads(x, o_ref, tq):
    for h in range(N_HEADS):
        o_ref[:, h * LANES:(h + 1) * LANES] = x[h * tq:(h + 1) * tq].astype(o_ref.dtype)


def _sb_prompt_kernel(q_ref, k_ref, v_ref, u_ref, o_ref, *, tq):
    i = pl.program_id(1)
    rows = N_HEADS * tq
    q = _stack_heads(q_ref)
    strict = (lax.broadcasted_iota(jnp.int32, (rows, tq), 1)
              < lax.broadcasted_iota(jnp.int32, (rows, tq), 0) % tq)

    def block(j, carry, acc, diag):
        start = pl.multiple_of(j * tq, tq)
        ls, lk = _stick_terms(_dot_nt(q, k_ref[pl.ds(start, tq), :]))
        if diag:
            lk = jnp.where(strict, lk, 0.0)
        after = _suffix_sums(lk, u_ref[...])
        if not diag:
            after = after + carry
        a = jnp.exp(ls + after)
        if diag:
            a = jnp.where(strict, a, 0.0)
        pv = _dot(a.astype(BF16), v_ref[pl.ds(start, tq), :])
        tot = jnp.sum(lk, axis=-1, keepdims=True)
        return (tot, pv) if diag else (carry + tot, acc + pv)

    def more(st):
        j, _, carry, acc = st
        carry, acc = block(j, carry, acc, False)
        return j - 1, jnp.max(carry), carry, acc

    carry, acc = block(i, None, None, True)
    _, _, _, acc = lax.while_loop(lambda st: (st[0] >= 0) & (st[1] > SB_NEGLIGIBLE), more,
                                  (i - 1, jnp.max(carry), carry, acc))
    _unstack_heads(acc, o_ref, tq)


def _suffix_matrix(tk):
    u = (np.arange(tk)[:, None] > np.arange(tk)[None, :]).astype(np.float32)
    return jnp.asarray(np.concatenate([u, u], axis=0), dtype=BF16)


def _sb_prompt(q, k, v, batch, seq, tq=256):
    nq = seq // tq
    return pl.pallas_call(
        functools.partial(_sb_prompt_kernel, tq=tq),
        grid=(batch, nq),
        in_specs=[pl.BlockSpec((tq, SLOT), lambda b, i: (b * nq + i, 0)),
                  pl.BlockSpec((seq, LANES), lambda b, i: (b, 0)),
                  pl.BlockSpec((seq, LANES), lambda b, i: (b, 0)),
                  _const_spec((2 * tq, tq))],
        out_specs=pl.BlockSpec((tq, SLOT), lambda b, i: (b * nq + i, 0)),
        out_shape=jax.ShapeDtypeStruct((batch * seq, SLOT), BF16),
        compiler_params=_cparams(("arbitrary", "arbitrary")),
        name="sb_prompt",
    )(q, k, v, _suffix_matrix(tq))


def _softmax_block(s, vext, c, state):
    mx = jnp.max(s, axis=-1, keepdims=True)
    if state is None:
        return mx, _dot(jnp.exp2((s - mx) * c).astype(BF16), vext)
    m_old, acc = state
    m_new = jnp.maximum(m_old, mx)
    p = jnp.exp2((s - m_new) * c).astype(BF16)
    return m_new, acc * jnp.exp2((m_old - m_new) * c) + _dot(p, vext)


def _softmax_result(state):
    acc = state[1]
    return acc[:, :LANES] / acc[:, LANES:]


def _mla_prompt_kernel(ql_ref, qr_ref, k_ref, o_ref, *, tq):
    i = pl.program_id(1)
    rows = N_HEADS * tq
    c = (MLA_NOPE + MLA_ROPE) ** -0.5 * LOG2E
    q = jnp.concatenate([_stack_heads(ql_ref), _stack_heads(qr_ref)], axis=1)
    causal = (lax.broadcasted_iota(jnp.int32, (rows, tq), 1)
              <= lax.broadcasted_iota(jnp.int32, (rows, tq), 0) % tq)
    value_lane = lax.broadcasted_iota(jnp.int32, (tq, 2 * LANES), 1) < KV_LORA

    def block(j, state):
        kb = k_ref[pl.ds(pl.multiple_of(j * tq, tq), tq), :]
        vext = jnp.where(value_lane, kb, jnp.ones_like(kb))
        s = _dot_nt(q, kb)
        if state is None:
            s = jnp.where(causal, s, NEG_INF)
        return _softmax_block(s, vext, c, state)

    state = lax.fori_loop(0, i, block, block(i, None))
    _unstack_heads(_softmax_result(state), o_ref, tq)


def _mla_prompt(ql, qr, kcat, batch, seq, tq=512):
    tq = math.gcd(tq, seq)
    nq = seq // tq
    qspec = pl.BlockSpec((tq, SLOT), lambda b, i: (b * nq + i, 0))
    return pl.pallas_call(
        functools.partial(_mla_prompt_kernel, tq=tq),
        grid=(batch, nq),
        in_specs=[qspec, qspec, pl.BlockSpec((seq, 2 * LANES), lambda b, i: (b, 0))],
        out_specs=qspec,
        out_shape=jax.ShapeDtypeStruct((batch * seq, SLOT), BF16),
        compiler_params=_cparams(("arbitrary", "arbitrary")),
        name="mla_prompt",
    )(ql, qr, kcat)


def _block_mean_kernel(k_ref, o_ref, *, nblk):
    for j in range(nblk):
        blk = k_ref[j * MOBA_BLOCK:(j + 1) * MOBA_BLOCK, :]
        o_ref[j:j + 1, :] = jnp.sum(blk, axis=0, keepdims=True) * (1.0 / MOBA_BLOCK)


def _block_mean(k):
    m = k.shape[0]
    nb = m // MOBA_BLOCK
    nblk = math.gcd(nb, 8)
    return pl.pallas_call(
        functools.partial(_block_mean_kernel, nblk=nblk),
        grid=(nb // nblk,),
        in_specs=[pl.BlockSpec((nblk * MOBA_BLOCK, LANES), lambda i: (i, 0))],
        out_specs=pl.BlockSpec((nblk, LANES), lambda i: (i, 0)),
        out_shape=jax.ShapeDtypeStruct((nb, LANES), F32),
        compiler_params=_cparams(("arbitrary",)),
        name="block_mean",
    )(k)


def _top_blocks(gate, n_valid, axis):
    pos = lax.broadcasted_iota(jnp.int32, gate.shape, axis)
    pos_f = pos.astype(F32)
    valid = pos < n_valid
    cur = jnp.where(valid, gate, NEG_INF)
    sel = jnp.zeros(gate.shape, F32)
    picks = []
    for _ in range(MOBA_TOPK):
        mx = jnp.max(cur, axis=axis, keepdims=True)
        idx = jnp.min(jnp.where(cur == mx, pos_f, 1e9), axis=axis, keepdims=True)
        pick = pos_f == idx
        sel = jnp.maximum(sel, jnp.where(pick & valid, 1.0, 0.0))
        cur = jnp.where(pick, NEG_INF, cur)
        picks.append(idx)
    return sel, picks


def _moba_prompt_kernel(q_ref, k_ref, v_ref, km_ref, o_ref, *, nbp):
    tq = MOBA_BLOCK
    i = pl.program_id(1)
    rows = N_HEADS * tq
    c = HEAD_DIM ** -0.5 * LOG2E
    causal = (lax.broadcasted_iota(jnp.int32, (rows, tq), 1)
              <= lax.broadcasted_iota(jnp.int32, (rows, tq), 0) % tq)
    lane = lax.broadcasted_iota(jnp.int32, (tq, LANES), 1)
    ones = jnp.ones((tq, LANES), BF16)

    qf = _stack_heads(q_ref)
    gate_t = _dot_nt_f32(km_ref[...], qf)[:nbp]
    sel_t, _ = _top_blocks(gate_t, i, 0)
    blk = lax.broadcasted_iota(jnp.int32, gate_t.shape, 0)
    bias_t = jnp.where((sel_t > 0.0) | (blk == i), 0.0, MASK_BIAS)
    bias = jnp.concatenate([bias_t, jnp.zeros((LANES - nbp, rows), F32)], axis=0).T
    q = jnp.concatenate([qf.astype(BF16), bias.astype(BF16)], axis=1)

    def block(j, state):
        start = pl.multiple_of(j * tq, tq)
        tag = jnp.where(lane == j, 1.0, 0.0).astype(BF16)
        s = _dot_nt(q, jnp.concatenate([k_ref[pl.ds(start, tq), :], tag], axis=1))
        if state is None:
            s = jnp.where(causal, s, NEG_INF)
        vext = jnp.concatenate([v_ref[pl.ds(start, tq), :], ones], axis=1)
        return _softmax_block(s, vext, c, state)

    state = lax.fori_loop(0, i, block, block(i, None))
    _unstack_heads(_softmax_result(state), o_ref, tq)


def _moba_prompt(q, k, v, kmean, batch, seq):
    nb = seq // MOBA_BLOCK
    assert nb <= LANES
    tq = MOBA_BLOCK
    kmean = jnp.pad(kmean.reshape(batch, nb, LANES), ((0, 0), (0, LANES - nb), (0, 0)))
    qspec = pl.BlockSpec((tq, SLOT), lambda b, i: (b * nb + i, 0))
    kv = pl.BlockSpec((seq, LANES), lambda b, i: (b, 0))
    return pl.pallas_call(
        functools.partial(_moba_prompt_kernel, nbp=-(-nb // 8) * 8),
        grid=(batch, nb),
        in_specs=[qspec, kv, kv, pl.BlockSpec((LANES, LANES), lambda b, i: (b, 0))],
        out_specs=qspec,
        out_shape=jax.ShapeDtypeStruct((batch * seq, SLOT), BF16),
        compiler_params=_cparams(("arbitrary", "arbitrary")),
        name="moba_prompt",
    )(q, k, v, kmean.reshape(batch * LANES, LANES))


class _Pager:
    def __init__(self, pt_ref, layer, caches, bufs, sems, pages_per_chunk):
        self.pt, self.layer, self.caches, self.bufs, self.sems = pt_ref, layer, caches, bufs, sems
        self.p = pages_per_chunk
        self.b, self.c = pl.program_id(0), pl.program_id(1)
        self.nb, self.nc = pl.num_programs(0), pl.num_programs(1)
        self.s = self.b * self.nc + self.c
        self.slot = self.s % 2

    def _copies(self, b, chunk, slot):
        out = []
        for cache, buf, sem in zip(self.caches, self.bufs, self.sems):
            for p in range(self.p):
                page = self.pt[b, chunk * self.p + p]
                out.append(pltpu.make_async_copy(cache.at[self.layer, page], buf.at[slot, p],
                                                 sem.at[slot]))
        return out

    def step(self):
        @pl.when(self.s == 0)
        def _():
            for cp in self._copies(self.b, self.c, self.slot):
                cp.start()

        last_chunk = self.c + 1 == self.nc
        b2 = jnp.where(last_chunk, jnp.where(self.b + 1 == self.nb, 0, self.b + 1), self.b)
        c2 = jnp.where(last_chunk, 0, self.c + 1)
        for cp in self._copies(b2, c2, 1 - self.slot):
            cp.start()
        for cp in self._copies(self.b, self.c, self.slot):
            cp.wait()
        return self.slot

    def finish(self):
        @pl.when(self.s + 1 == self.nb * self.nc)
        def _():
            for cp in self._copies(0, 0, 1 - self.slot):
                cp.wait()


def _sb_sample_kernel(pt_ref, q_ref, u_ref, k_hbm, v_hbm, o_ref, kbuf, vbuf, ksem, vsem, *, layer,
                      ppc):
    b, nb = pl.program_id(0), pl.num_programs(0)
    nchunks = pt_ref.shape[1] // ppc
    home = b % 2

    def copies(bb, chunk, slot):
        base = (nchunks - 1 - chunk) * ppc
        return [pltpu.make_async_copy(cache.at[layer, pt_ref[bb, base + p]], buf.at[slot, p],
                                      sem.at[slot])
                for cache, buf, sem in ((k_hbm, kbuf, ksem), (v_hbm, vbuf, vsem))
                for p in range(ppc)]

    @pl.when(b == 0)
    def _():
        for cp in copies(b, 0, home):
            cp.start()

    @pl.when(b + 1 < nb)
    def _():
        for cp in copies(b + 1, 0, 1 - home):
            cp.start()

    q = q_ref[0]

    def chunk(carry, acc):
        z = jnp.concatenate([_dot(q, kbuf[home, p].astype(BF16)) for p in range(ppc)], axis=0)
        ls, lk = _stick_terms(z)
        within = _suffix_sums(lk, u_ref[...])
        tot = jnp.sum(lk, axis=-1, keepdims=True)
        for p in reversed(range(ppc)):
            r = slice(p * N_HEADS, (p + 1) * N_HEADS)
            a = jnp.exp(ls[r] + within[r] + carry)
            acc = acc + _dot_nt(a.astype(BF16), vbuf[home, p].astype(BF16))
            carry = carry + tot[r]
        return carry, acc

    for cp in copies(b, 0, home):
        cp.wait()
    carry, acc = chunk(jnp.zeros((N_HEADS, 1), F32), jnp.zeros((N_HEADS, LANES), F32))

    def more(st):
        c, _, carry, acc = st
        for cp in copies(b, c, home):
            cp.start()
        for cp in copies(b, c, home):
            cp.wait()
        carry, acc = chunk(carry, acc)
        return c + 1, jnp.max(carry), carry, acc

    _, _, _, acc = lax.while_loop(lambda st: (st[0] < nchunks) & (st[1] > SB_NEGLIGIBLE), more,
                                  (1, jnp.max(carry), carry, acc))
    o_ref[0] = acc.astype(BF16)


def _sb_sample(pt, q, cache_k, cache_v, layer, ppc=8):
    nseq, npages = pt.shape
    assert npages % ppc == 0
    head = pl.BlockSpec((1, N_HEADS, LANES), lambda b, pt: (b, 0, 0))
    out = pl.pallas_call(
        functools.partial(_sb_sample_kernel, layer=layer, ppc=ppc),
        grid_spec=pltpu.PrefetchScalarGridSpec(
            num_scalar_prefetch=1,
            grid=(nseq,),
            in_specs=[head, pl.BlockSpec((2 * PAGE, PAGE), lambda b, pt: (0, 0)),
                      pl.BlockSpec(memory_space=pl.ANY), pl.BlockSpec(memory_space=pl.ANY)],
            out_specs=head,
            scratch_shapes=[pltpu.VMEM((2, ppc, LANES, PAGE), F32),
                            pltpu.VMEM((2, ppc, LANES, PAGE), F32),
                            pltpu.SemaphoreType.DMA((2,)), pltpu.SemaphoreType.DMA((2,))]),
        out_shape=jax.ShapeDtypeStruct((nseq, N_HEADS, LANES), BF16),
        compiler_params=_cparams(("arbitrary",)),
        name="sb_sample",
    )(pt, q.reshape(nseq, N_HEADS, LANES), _suffix_matrix(PAGE), cache_k, cache_v)
    return out.reshape(nseq, SLOT)


def _stream_sample_kernel(pt_ref, ql_ref, qr_ref, cnew_ref, rnew_ref, mq_ref, c_hbm, r_hbm, mk_hbm,
                          o_ref, sel_ref, cbuf, rbuf, mkbuf, csem, rsem, mksem, m_ref, l_ref,
                          acc_ref, km_ref, *, layer, ppc, n_blocks):
    c, nc = pl.program_id(1), pl.num_programs(1)
    pager = _Pager(pt_ref, layer, (c_hbm, r_hbm, mk_hbm), (cbuf, rbuf, mkbuf),
                   (csem, rsem, mksem), ppc)
    slot = pager.step()

    @pl.when(c == 0)
    def _():
        km_ref[...] = jnp.zeros_like(km_ref)
        m_ref[...] = jnp.full_like(m_ref, NEG_INF)
        l_ref[...] = jnp.zeros_like(l_ref)
        acc_ref[...] = jnp.zeros_like(acc_ref)

    km = _moba_means_chunk(c, slot, mkbuf, km_ref, ppc)
    ql, qr, m, l, acc = _mla_sample_chunk(slot, ql_ref, qr_ref, cbuf, rbuf, m_ref, l_ref, acc_ref,
                                          ppc)

    @pl.when(c == nc - 1)
    def _():
        _moba_pick_blocks(mq_ref, km, sel_ref, n_blocks)
        _mla_sample_finish(ql, qr, m, l, acc, cnew_ref, rnew_ref, o_ref)

    pager.finish()


def _mla_sample_chunk(slot, ql_ref, qr_ref, cbuf, rbuf, m_ref, l_ref, acc_ref, ppc):
    scale = (MLA_NOPE + MLA_ROPE) ** -0.5
    ql = ql_ref[0]
    qr = qr_ref[0][:, :MLA_ROPE]
    pages = [cbuf[slot, p].astype(BF16) for p in range(ppc)]
    s = [(_dot_nt(ql, pages[p]) + _dot(qr, rbuf[slot, p].astype(BF16))) * scale
         for p in range(ppc)]
    m_old = m_ref[...]
    m_new = jnp.maximum(m_old, jnp.max(functools.reduce(jnp.maximum, s), axis=-1, keepdims=True))
    alpha = jnp.exp(m_old - m_new)
    psum = jnp.zeros((N_HEADS, LANES), F32)
    acc = acc_ref[...] * alpha
    for p in range(ppc):
        pr = jnp.exp(s[p] - m_new)
        psum = psum + pr
        acc = acc + _dot(pr.astype(BF16), pages[p])
    l = l_ref[...] * alpha + jnp.sum(psum, axis=-1, keepdims=True)
    m_ref[...] = m_new
    l_ref[...] = l
    acc_ref[...] = acc
    return ql, qr, m_new, l, acc


def _mla_sample_finish(ql, qr, m, l, acc, cnew_ref, rnew_ref, o_ref):
    scale = (MLA_NOPE + MLA_ROPE) ** -0.5
    cnew = cnew_ref[0].astype(BF16).astype(F32)
    rnew = rnew_ref[0].astype(BF16).astype(F32)
    s_new = (jnp.sum(ql.astype(F32) * cnew, axis=-1, keepdims=True)
             + jnp.sum(qr.astype(F32) * rnew, axis=-1, keepdims=True)) * scale
    m_fin = jnp.maximum(m, s_new)
    a2 = jnp.exp(m - m_fin)
    p_new = jnp.exp(s_new - m_fin)
    o = (acc * a2 + p_new.astype(BF16).astype(F32) * cnew) / (l * a2 + p_new)
    o_ref[0] = o.astype(BF16)


def _stream_sample(pt, ql, qr, ckv_new, kr_new, mq, cache_c, cache_r, cache_mk, layer, ppc=64):
    nseq, npages = pt.shape
    ppc = math.gcd(ppc, npages)
    n_blocks = npages * PAGE // MOBA_BLOCK
    assert npages % ppc == 0 and MOBA_TOPK <= n_blocks <= LANES
    caches = [cache_c, cache_r, cache_mk]
    head = pl.BlockSpec((1, N_HEADS, LANES), lambda b, c, pt: (b, 0, 0))
    new = lambda w: pl.BlockSpec((1, 1, w), lambda b, c, pt: (b, 0, 0))
    out, sel = pl.pallas_call(
        functools.partial(_stream_sample_kernel, layer=layer, ppc=ppc, n_blocks=n_blocks),
        grid_spec=pltpu.PrefetchScalarGridSpec(
            num_scalar_prefetch=1,
            grid=(nseq, npages // ppc),
            in_specs=[head, head, new(KV_LORA), new(MLA_ROPE), head]
            + [pl.BlockSpec(memory_space=pl.ANY) for _ in caches],
            out_specs=[head, head],
            scratch_shapes=[pltpu.VMEM((2, ppc) + c.shape[2:], c.dtype) for c in caches]
            + [pltpu.SemaphoreType.DMA((2,)) for _ in caches]
            + [pltpu.VMEM((N_HEADS, 1), F32), pltpu.VMEM((N_HEADS, 1), F32),
               pltpu.VMEM((N_HEADS, LANES), F32), pltpu.VMEM((LANES, LANES), F32)]),
        out_shape=[jax.ShapeDtypeStruct((nseq, N_HEADS, LANES), BF16),
                   jax.ShapeDtypeStruct((nseq, N_HEADS, LANES), jnp.int32)],
        compiler_params=_cparams(("arbitrary", "arbitrary")),
        name="stream_sample",
    )(pt, ql.reshape(nseq, N_HEADS, LANES), qr.reshape(nseq, N_HEADS, LANES),
      ckv_new.reshape(nseq, 1, KV_LORA), kr_new.reshape(nseq, 1, MLA_ROPE),
      mq.reshape(nseq, N_HEADS, LANES), *caches)
    return out.reshape(nseq, SLOT), sel


def _moba_means_chunk(c, slot, kbuf, km_ref, ppc):
    ppb = MOBA_BLOCK // PAGE
    bpc = ppc // ppb
    lane_blk = lax.broadcasted_iota(jnp.int32, (LANES, LANES), 1)
    km = km_ref[...]
    for j in range(bpc):
        tot = kbuf[slot, j * ppb]
        for t in range(1, ppb):
            tot = tot + kbuf[slot, j * ppb + t]
        mean = jnp.sum(tot, axis=1, keepdims=True) * (1.0 / MOBA_BLOCK)
        km = jnp.where(lane_blk == c * bpc + j, mean, km)
    km_ref[...] = km
    return km


def _moba_pick_blocks(q_ref, km, o_ref, n_blocks):
    gate = _dot_f32(q_ref[0], km)
    _, picks = _top_blocks(gate, n_blocks, 1)
    lane = lax.broadcasted_iota(jnp.int32, (N_HEADS, LANES), 1)
    out = jnp.zeros((N_HEADS, LANES), F32)
    for r, idx in enumerate(picks):
        out = jnp.where(lane == r, idx, out)
    o_ref[0] = out.astype(jnp.int32)


def _moba_sample_kernel(pg_ref, q_ref, knew_ref, vnew_ref, k_hbm, v_hbm, o_ref, kbuf, vbuf, ksem,
                        vsem, *, layer, npg):
    b, nb = pl.program_id(0), pl.num_programs(0)
    slot = b % 2
    per_head = npg // N_HEADS
    scale = HEAD_DIM ** -0.5

    def copies(bb, sl):
        out = []
        for cache, buf, sem in ((k_hbm, kbuf, ksem), (v_hbm, vbuf, vsem)):
            for p in range(npg):
                out.append(pltpu.make_async_copy(cache.at[layer, pg_ref[bb, p]], buf.at[sl, p],
                                                 sem.at[sl]))
        return out

    @pl.when(b == 0)
    def _():
        for cp in copies(b, slot):
            cp.start()

    nxt = jnp.where(b + 1 == nb, 0, b + 1)
    for cp in copies(nxt, 1 - slot):
        cp.start()
    for cp in copies(b, slot):
        cp.wait()

    qf = q_ref[0]
    qb = qf.astype(BF16)
    row = lax.broadcasted_iota(jnp.int32, (N_HEADS, LANES), 0)
    s = []
    for p in range(npg):
        sp = _dot(qb, kbuf[slot, p].astype(BF16)) * scale
        s.append(jnp.where(row == p // per_head, sp, NEG_INF))
    knew = knew_ref[0].astype(BF16).astype(F32)
    s_new = jnp.sum(qb.astype(F32) * knew, axis=-1, keepdims=True) * scale
    m = jnp.maximum(jnp.max(functools.reduce(jnp.maximum, s), axis=-1, keepdims=True), s_new)
    p_new = jnp.exp(s_new - m)
    psum = jnp.zeros((N_HEADS, LANES), F32)
    acc = p_new.astype(BF16).astype(F32) * vnew_ref[0].astype(BF16).astype(F32)
    for p in range(npg):
        pr = jnp.exp(s[p] - m)
        psum = psum + pr
        acc = acc + _dot_nt(pr.astype(BF16), vbuf[slot, p].astype(BF16))
    l = jnp.sum(psum, axis=-1, keepdims=True) + p_new
    o_ref[0] = (acc / l).astype(BF16)

    @pl.when(b + 1 == nb)
    def _():
        for cp in copies(0, 1 - slot):
            cp.wait()


def _moba_sample(pages, q, k_new, v_new, cache_k, cache_v, layer):
    nseq, npg = pages.shape
    head = pl.BlockSpec((1, N_HEADS, LANES), lambda b, pg: (b, 0, 0))
    new = pl.BlockSpec((1, 1, LANES), lambda b, pg: (b, 0, 0))
    out = pl.pallas_call(
        functools.partial(_moba_sample_kernel, layer=layer, npg=npg),
        grid_spec=pltpu.PrefetchScalarGridSpec(
            num_scalar_prefetch=1,
            grid=(nseq,),
            in_specs=[head, new, new, pl.BlockSpec(memory_space=pl.ANY),
                      pl.BlockSpec(memory_space=pl.ANY)],
            out_specs=head,
            scratch_shapes=[pltpu.VMEM((2, npg, PAGE, LANES), F32),
                            pltpu.VMEM((2, npg, PAGE, LANES), F32),
                            pltpu.SemaphoreType.DMA((2,)), pltpu.SemaphoreType.DMA((2,))]),
        out_shape=jax.ShapeDtypeStruct((nseq, N_HEADS, LANES), BF16),
        compiler_params=_cparams(("arbitrary",)),
        name="moba_sample",
    )(pages, q.reshape(nseq, N_HEADS, LANES), k_new.reshape(nseq, 1, LANES),
      v_new.reshape(nseq, 1, LANES), cache_k, cache_v)
    return out.reshape(nseq, SLOT)


def _merge_kernel(x_ref, osb_ref, olat_ref, omb_ref, g_ref, wsb_ref, wuv_ref, wmla_ref, wmb_ref,
                  wout_ref, o_ref):
    d = D_MODEL
    a = _dot(osb_ref[...], wsb_ref[...])
    b = _dot(_dot(olat_ref[...], wuv_ref[...]).astype(BF16), wmla_ref[...])
    c = _dot(omb_ref[...], wmb_ref[...])
    merged = g_ref[:, :d] * a + g_ref[:, d:2 * d] * b + g_ref[:, 2 * d:] * c
    o_ref[...] = x_ref[...] + _dot(merged.astype(BF16), wout_ref[...])


def _merge(x, osb, olat, omb, gates, wsb, wuv, wmla, wmb, wout, tm):
    m = x.shape[0]
    row = lambda w: pl.BlockSpec((tm, w), lambda i: (i, 0))
    return pl.pallas_call(
        _merge_kernel,
        grid=(m // tm,),
        in_specs=[row(D_MODEL), row(SLOT), row(SLOT), row(SLOT), row(N_BRANCH * D_MODEL),
                  _const_spec(wsb.shape), _const_spec(wuv.shape), _const_spec(wmla.shape),
                  _const_spec(wmb.shape), _const_spec(wout.shape)],
        out_specs=row(D_MODEL),
        out_shape=jax.ShapeDtypeStruct((m, D_MODEL), F32),
        compiler_params=_cparams(("arbitrary",)),
        name="merge",
    )(x, osb, olat, omb, gates, wsb, wuv, wmla, wmb, wout)


def _pad_heads(w, group_of_head=True):
    lead = w.shape[:-1]
    w = w.reshape(*lead, N_HEADS, 1, HEAD_DIM)
    half = (jnp.arange(N_HEADS)[:, None] // GROUP == jnp.arange(KV_HEADS)[None, :]).astype(w.dtype)
    return (w * half[:, :, None]).reshape(*lead, SLOT)


def _layer_weights(l, w_in, w_uk, w_uv, w_o_sb, w_o_mla, w_o_moba, w_out):
    w = w_in[l]
    d = D_MODEL
    o = np.cumsum([0, 512, 128, 128, 768, 128, 32, 512, 128, 128, 3072])
    sbq, sbk, sbv, mq, ckv, kr, mbq, mbk, mbv, gates = [w[:, o[i]:o[i + 1]] for i in range(10)]
    mq = mq.reshape(d, N_HEADS, MLA_NOPE + MLA_ROPE)
    qnope = mq[:, :, :MLA_NOPE].reshape(d, N_HEADS * MLA_NOPE)
    qrope = jnp.pad(mq[:, :, MLA_NOPE:], ((0, 0), (0, 0), (0, LANES - MLA_ROPE))).reshape(d, SLOT)
    kr = jnp.pad(kr, ((0, 0), (0, LANES - MLA_ROPE)))
    w_all = jnp.concatenate([_pad_heads(sbq), sbk, sbv, qnope, qrope, ckv, kr, _pad_heads(mbq),
                             mbk, mbv, gates], axis=1).astype(BF16)
    eye = jnp.eye(N_HEADS, dtype=F32)
    wuk_bd = (w_uk[l].transpose(1, 2, 0)[:, :, None, :] * eye[:, None, :, None]).reshape(
        N_HEADS * MLA_NOPE, SLOT).astype(BF16)
    wuv_bd = (w_uv[l].transpose(1, 0, 2)[:, :, None, :] * eye[:, None, :, None]).reshape(
        SLOT, N_HEADS * HEAD_DIM).astype(BF16)
    wsb = _pad_heads(w_o_sb[l].T).T.astype(BF16)
    wmb = _pad_heads(w_o_moba[l].T).T.astype(BF16)
    return dict(w_all=w_all, wuk=wuk_bd, wuv=wuv_bd, wsb=wsb, wmla=w_o_mla[l].astype(BF16), wmb=wmb,
                wout=w_out[l].astype(BF16))


def _rope_tables(pos):
    def one(dim):
        half = dim // 2
        inv = jnp.exp(-math.log(ROPE_THETA) * 2.0 * jnp.arange(half, dtype=F32) / dim)
        ang = pos.astype(F32)[:, None] * inv[None, :]
        cos, sin = jnp.cos(ang), jnp.sin(ang)
        reps = LANES // dim
        return (jnp.tile(jnp.concatenate([cos, cos], axis=1), (1, reps)),
                jnp.tile(jnp.concatenate([-sin, sin], axis=1), (1, reps)))
    c64, s64 = one(HEAD_DIM)
    c32, s32 = one(MLA_ROPE)
    return c64, s64, c32, s32


def kernel(x_prompt, x_sample, cache_sb_k, cache_sb_v, cache_mla_ckv, cache_mla_krope, cache_moba_k, cache_moba_v, page_table, g_ffn1, w_ffn1_gu, w_ffn1_down, g_mix, w_in, g_kv, w_uk, w_uv, w_o_sb, w_o_mla, w_o_moba, w_out, g_ffn2, w_ffn2_gu, w_ffn2_down, g_final):
    batch, seq, d = x_prompt.shape
    nseq, dec_seq, _ = x_sample.shape
    assert dec_seq == 1 and d == D_MODEL
    depth = w_in.shape[0]
    n_pool = cache_sb_k.shape[1]
    npages = page_table.shape[1]
    past_len = npages * PAGE
    assert past_len % MOBA_BLOCK == 0 and seq % MOBA_BLOCK == 0
    mp, ms = batch * seq, nseq
    tm_p, tm_s = 512, ms

    tabs_p = _rope_tables(jnp.arange(seq, dtype=jnp.int32))
    tabs_s = _rope_tables(jnp.full((ms,), past_len, dtype=jnp.int32))
    feat_major = lambda c: jnp.transpose(c, (0, 1, 3, 4, 2)).reshape(depth, n_pool, LANES, PAGE)
    csb_k, csb_v = feat_major(cache_sb_k), feat_major(cache_sb_v)
    cmb_k, cmb_v = feat_major(cache_moba_k), feat_major(cache_moba_v)
    ckr = jnp.transpose(cache_mla_krope, (0, 1, 3, 2))
    gf = g_final.reshape(1, d)

    hp = x_prompt.reshape(mp, d)
    hs = x_sample.reshape(ms, d)
    rows_p, rows_s = [], []
    for l in range(depth):
        lw = _layer_weights(l, w_in, w_uk, w_uv, w_o_sb, w_o_mla, w_o_moba, w_out)
        f1 = (g_ffn1[l].reshape(1, d), w_ffn1_gu[l][:, :D_FF].astype(BF16),
              w_ffn1_gu[l][:, D_FF:].astype(BF16), w_ffn1_down[l].astype(BF16), gf)
        f2 = (g_ffn2[l].reshape(1, d), w_ffn2_gu[l][:, :D_FF].astype(BF16),
              w_ffn2_gu[l][:, D_FF:].astype(BF16), w_ffn2_down[l].astype(BF16), gf)
        last = l == depth - 1
        gm, gkv = g_mix[l].reshape(1, d), g_kv[l].reshape(1, KV_LORA)
        merge_w = (lw["wsb"], lw["wuv"], lw["wmla"], lw["wmb"], lw["wout"])

        hp = _ffn(hp, *f1, tm_p, False)
        (sbq, sbk, sbv, sbkb, sbvb, qlat, qrope, ckv, kr, kcat, mbq, mbk, mbv, mbkb, mbvb,
         gates) = _proj(hp, gm, lw["w_all"], gkv, lw["wuk"], tabs_p, tm_p, seq // tm_p)
        osb = _sb_prompt(sbq, sbkb, sbvb, batch, seq)
        olat = _mla_prompt(qlat, qrope, kcat, batch, seq)
        kmean = _block_mean(mbk)
        omb = _moba_prompt(mbq, mbkb, mbvb, kmean, batch, seq)
        hp = _merge(hp, osb, olat, omb, gates, *merge_w, tm_p)
        hp = _ffn(hp, *f2, tm_p, last)
        rows_p.append((sbk.reshape(batch, seq, KV_HEADS, HEAD_DIM),
                       sbv.reshape(batch, seq, KV_HEADS, HEAD_DIM),
                       ckv.reshape(batch, seq, KV_LORA), kr.reshape(batch, seq, MLA_ROPE),
                       mbk.reshape(batch, seq, KV_HEADS, HEAD_DIM),
                       mbv.reshape(batch, seq, KV_HEADS, HEAD_DIM)))

        hs = _ffn(hs, *f1, tm_s, False)
        (sbq, sbk, sbv, _, _, qlat, qrope, ckv, kr, _, mbq, mbk, mbv, _, _,
         gates) = _proj(hs, gm, lw["w_all"], gkv, lw["wuk"], tabs_s, tm_s, 1)
        osb = _sb_sample(page_table, sbq, csb_k, csb_v, l)
        olat, sel = _stream_sample(page_table, qlat, qrope, ckv, kr, mbq, cache_mla_ckv, ckr, cmb_k, l)
        sel = sel[:, :, :MOBA_TOPK]
        ppb = MOBA_BLOCK // PAGE
        page_idx = (sel[..., None] * ppb + jnp.arange(ppb, dtype=jnp.int32)).reshape(nseq, -1)
        pages = jnp.take_along_axis(page_table, page_idx, axis=1)
        omb = _moba_sample(pages, mbq, mbk, mbv, cmb_k, cmb_v, l)
        hs = _merge(hs, osb, olat, omb, gates, *merge_w, tm_s)
        hs = _ffn(hs, *f2, tm_s, last)
        rows_s.append((sbk.reshape(nseq, 1, KV_HEADS, HEAD_DIM),
                       sbv.reshape(nseq, 1, KV_HEADS, HEAD_DIM),
                       ckv.reshape(nseq, 1, KV_LORA), kr.reshape(nseq, 1, MLA_ROPE),
                       mbk.reshape(nseq, 1, KV_HEADS, HEAD_DIM),
                       mbv.reshape(nseq, 1, KV_HEADS, HEAD_DIM)))

    y_prompt = hp.reshape(batch, seq, d)
    y_sample = hs.reshape(nseq, 1, d)
    stacked_p = [jnp.stack(r) for r in zip(*rows_p)]
    stacked_s = [jnp.stack(r) for r in zip(*rows_s)]
    return (y_prompt, y_sample, *stacked_p, *stacked_s)
```
